```python
import math
import jax
import jax.numpy as jnp
from jax import lax
import numpy as np

D_MODEL = 1024
BATCH = 2
SEQ = 16384
DEPTH = 4

GRID_W = 64
CTX_LEN = 256
D_FF = 2816
N_MOD = 9
NORM_EPS = 1e-6
NA_HEADS = 8
NA_HEAD_DIM = 32
NA_ROWS = 8
NA_COLS = 16
RET_HEADS = 4
RET_QK_DIM = 64
RET_V_DIM = 128
RET_CHUNK = 128
ROPE_BASE = 10000.0
S5_GROUPS = 16
S5_GROUP_CH = 16
S5_STATE = 64
NA_WIDTH = NA_HEADS * NA_HEAD_DIM
RET_QK_WIDTH = RET_HEADS * RET_QK_DIM
RET_V_WIDTH = RET_HEADS * RET_V_DIM
S5_WIDTH = S5_GROUPS * S5_GROUP_CH
MIX_WIDTH = NA_WIDTH + RET_V_WIDTH + S5_WIDTH
IN_SPLITS = (NA_WIDTH, NA_WIDTH, NA_WIDTH, RET_QK_WIDTH, RET_QK_WIDTH, RET_V_WIDTH, RET_V_WIDTH, S5_WIDTH)
IN_WIDTH = sum(IN_SPLITS)

kernel_name = "hybrid_natten_retnet_s5_macaron_block"


def rms_norm(x):
    xf = x.astype(jnp.float32)
    return (xf * lax.rsqrt(jnp.mean(xf * xf, axis=-1, keepdims=True) + NORM_EPS)).astype(x.dtype)


def modulate(h, shift, scale):
    return h * (1 + scale) + shift


def swiglu(h, w_in, w_out):
    a, b = jnp.split(h @ w_in, 2, axis=-1)
    return (jax.nn.silu(a) * b) @ w_out


def to_heads(z, n_heads):
    b, t, _ = z.shape
    return z.reshape(b, t, n_heads, -1).transpose(0, 2, 1, 3)


def from_heads(z):
    b, h, t, d = z.shape
    return z.transpose(0, 2, 1, 3).reshape(b, t, h * d)


def axial_rope(z):
    n, d = z.shape[-2], z.shape[-1]
    nf = d // 4
    inv = ROPE_BASE ** (-jnp.arange(nf, dtype=jnp.float32) / nf)
    t = jnp.arange(n)
    row = (t // GRID_W).astype(jnp.float32)
    col = (t % GRID_W).astype(jnp.float32)
    ang = jnp.concatenate([row[:, None] * inv, col[:, None] * inv], axis=-1)
    cos = jnp.cos(ang).astype(z.dtype)
    sin = jnp.sin(ang).astype(z.dtype)
    z1, z2 = jnp.split(z, 2, axis=-1)
    return jnp.concatenate([z1 * cos - z2 * sin, z1 * sin + z2 * cos], axis=-1)


def neighborhood_attention(q, k, v, kc, vc, rpb, rows):
    b, h, n, dh = q.shape
    kr = min(NA_ROWS, rows)
    qg = q.reshape(b, h, rows, GRID_W, dh)
    kg = k.reshape(b, h, rows, GRID_W, dh)
    vg = v.reshape(b, h, rows, GRID_W, dh)
    col_start = np.clip(np.arange(GRID_W) - NA_COLS // 2, 0, GRID_W - NA_COLS)
    col_idx = col_start[:, None] + np.arange(NA_COLS)[None, :]
    col_off = col_idx - np.arange(GRID_W)[:, None] + NA_COLS - 1
    scale = dh ** -0.5
    n_loc = kr * NA_COLS

    def one_row(r):
        rs = jnp.clip(r - kr // 2, 0, rows - kr)
        k_win = lax.dynamic_slice_in_dim(kg, rs, kr, axis=2)[:, :, :, col_idx]
        v_win = lax.dynamic_slice_in_dim(vg, rs, kr, axis=2)[:, :, :, col_idx]
        q_row = lax.dynamic_index_in_dim(qg, r, axis=2, keepdims=False)
        row_off = rs + jnp.arange(kr) - r + NA_ROWS - 1
        bias = jnp.transpose(rpb[:, row_off][:, :, col_off], (0, 2, 1, 3))
        s_loc = jnp.einsum('bhwd,bhiwjd->bhwij', q_row, k_win).astype(jnp.float32) * scale
        s_loc = s_loc + bias[None].astype(jnp.float32)
        s_ctx = jnp.einsum('bhwd,bhld->bhwl', q_row, kc).astype(jnp.float32) * scale
        s = jnp.concatenate([s_loc.reshape(b, h, GRID_W, n_loc), s_ctx], axis=-1)
        p = jax.nn.softmax(s, axis=-1).astype(v.dtype)
        p_loc = p[..., :n_loc].reshape(b, h, GRID_W, kr, NA_COLS)
        p_ctx = p[..., n_loc:]
        return (jnp.einsum('bhwij,bhiwjd->bhwd', p_loc, v_win)
                + jnp.einsum('bhwl,bhld->bhwd', p_ctx, vc))

    out = lax.map(one_row, jnp.arange(rows))
    return jnp.transpose(out, (1, 0, 3, 2, 4)).reshape(b, n, h * dh)


def context_attention(qc, kc, vc):
    scale = qc.shape[-1] ** -0.5
    s = jnp.einsum('bhqd,bhkd->bhqk', qc, kc).astype(jnp.float32) * scale
    p = jax.nn.softmax(s, axis=-1).astype(vc.dtype)
    return from_heads(jnp.einsum('bhqk,bhkd->bhqd', p, vc))


def retention_chunkwise(q, k, v, log_g, s0, inclusive):
    f32 = jnp.float32
    b, h, t, dk = q.shape
    dv = v.shape[-1]
    nc = t // RET_CHUNK
    qc = q.astype(f32).reshape(b, h, nc, RET_CHUNK, dk)
    kc = k.astype(f32).reshape(b, h, nc, RET_CHUNK, dk)
    vc = v.astype(f32).reshape(b, h, nc, RET_CHUNK, dv)
    pos = jnp.arange(RET_CHUNK, dtype=f32)
    diff = pos[:, None] - pos[None, :]
    mask = (diff >= 0) if inclusive else (diff > 0)
    lg = log_g[:, None, None]
    dmat = jnp.where(mask, jnp.exp(lg * jnp.where(mask, diff, 0.0)), 0.0)
    scores = jnp.einsum('bhnid,bhnjd->bhnij', qc, kc) * dmat[None, :, None]
    o_intra = jnp.einsum('bhnij,bhnjv->bhniv', scores, vc)
    w_end = jnp.exp(log_g[:, None] * (RET_CHUNK - 1 - pos))
    kv = jnp.einsum('bhnjd,hj,bhnjv->bhndv', kc, w_end, vc)
    g_chunk = jnp.exp(log_g * RET_CHUNK)[:, None, None]
    if s0 is None:
        s0 = jnp.zeros((b, h, dk, dv), f32)

    def step(s, kv_n):
        return g_chunk * s + kv_n, s

    s_last, s_prev = lax.scan(step, s0.astype(f32), jnp.moveaxis(kv, 2, 0))
    w_in = jnp.exp(log_g[:, None] * (pos + 1.0))
    o_cross = jnp.einsum('bhnid,nbhdv->bhniv', qc, s_prev) * w_in[None, :, None, :, None]
    return (o_intra + o_cross).reshape(b, h, t, dv), s_last


def retention_final_state(k, v, log_g):
    t = k.shape[2]
    w = jnp.exp(log_g[:, None] * (t - 1 - jnp.arange(t, dtype=jnp.float32)))
    return jnp.einsum('bhtd,ht,bhtv->bhdv', k.astype(jnp.float32), w, v.astype(jnp.float32))


def retention_output(o, gate, gn_gain):
    mu = jnp.mean(o, axis=-1, keepdims=True)
    var = jnp.mean(jnp.square(o - mu), axis=-1, keepdims=True)
    o = from_heads((o - mu) * lax.rsqrt(var + NORM_EPS)) * gn_gain.astype(jnp.float32)
    return (jax.nn.silu(gate.astype(jnp.float32)) * o).astype(gate.dtype)


def retention_mixer(q, k, v, gate, qc, kc, vc, gate_c, decay, gn_gain, need_ctx):
    log_g = jax.nn.log_sigmoid(decay.astype(jnp.float32))
    flip = lambda z: jnp.flip(z, axis=2)
    y_ctx = None
    if need_ctx:
        oc_f, s_f = retention_chunkwise(qc, kc, vc, log_g[0], None, True)
        oc_b, s_b = retention_chunkwise(flip(qc), flip(kc), flip(vc), log_g[1], None, False)
        y_ctx = retention_output(oc_f + flip(oc_b), gate_c, gn_gain)
    else:
        s_f = retention_final_state(kc, vc, log_g[0])
        s_b = retention_final_state(flip(kc), flip(vc), log_g[1])
    o_f, _ = retention_chunkwise(q, k, v, log_g[0], s_f, True)
    o_b, _ = retention_chunkwise(flip(q), flip(k), flip(v), log_g[1], s_b, False)
    y = retention_output(o_f + flip(o_b), gate, gn_gain)
    return y, y_ctx


def s5_discretize(a_re, a_im, log_dt, b_re, b_im):
    f32 = jnp.float32
    a_re = jnp.minimum(a_re.astype(f32), -1e-4)
    a_im = a_im.astype(f32)
    dt = jnp.exp(log_dt.astype(f32))[..., None]
    mag = jnp.exp(dt * a_re)
    ab_re = mag * jnp.cos(dt * a_im)
    ab_im = mag * jnp.sin(dt * a_im)
    den = a_re * a_re + a_im * a_im
    nr = ab_re - 1.0
    f_re = ((nr * a_re + ab_im * a_im) / den)[..., None]
    f_im = ((ab_im * a_re - nr * a_im) / den)[..., None]
    br = b_re.astype(f32)[None]
    bi = b_im.astype(f32)[None]
    bb_re = f_re * br - f_im * bi
    bb_im = f_re * bi + f_im * br
    return ab_re, ab_im, bb_re, bb_im


def _ssm_combine(e1, e2):
    ar1, ai1, br1, bi1 = e1
    ar2, ai2, br2, bi2 = e2
    return (ar1 * ar2 - ai1 * ai2,
            ar1 * ai2 + ai1 * ar2,
            ar2 * br1 - ai2 * bi1 + br2,
            ar2 * bi1 + ai2 * br1 + bi2)


def s5_scan(u, ab_re, ab_im, bb_re, bb_im, s0_re, s0_im):
    bu_re = jnp.einsum('tbgc,gpc->tbgp', u, bb_re)
    bu_im = jnp.einsum('tbgc,gpc->tbgp', u, bb_im)
    if s0_re is not None:
        bu_re = bu_re.at[0].add(ab_re * s0_re - ab_im * s0_im)
        bu_im = bu_im.at[0].add(ab_re * s0_im + ab_im * s0_re)
    shape = (u.shape[0], 1) + ab_re.shape
    a_re = jnp.broadcast_to(ab_re, shape)
    a_im = jnp.broadcast_to(ab_im, shape)
    _, _, x_re, x_im = lax.associative_scan(_ssm_combine, (a_re, a_im, bu_re, bu_im), axis=0)
    return x_re, x_im


def s5_readout(x_re, x_im, c_re, c_im):
    return (jnp.einsum('tbgp,gcp->tbgc', x_re, c_re.astype(jnp.float32))
            - jnp.einsum('tbgp,gcp->tbgc', x_im, c_im.astype(jnp.float32)))


def s5_glu(y, w_glu):
    a, g = jnp.split(jax.nn.gelu(y).astype(w_glu.dtype) @ w_glu, 2, axis=-1)
    return a * jax.nn.sigmoid(g)


def s5_mixer(u, uc, a_re, a_im, log_dt, b_re, b_im, c_re, c_im, d, w_glu, need_ctx):
    b, n, _ = u.shape
    l = uc.shape[1]
    ut = u.astype(jnp.float32).reshape(b, n, S5_GROUPS, S5_GROUP_CH).transpose(1, 0, 2, 3)
    uct = uc.astype(jnp.float32).reshape(b, l, S5_GROUPS, S5_GROUP_CH).transpose(1, 0, 2, 3)
    ab_re, ab_im, bb_re, bb_im = s5_discretize(a_re, a_im, log_dt, b_re, b_im)
    d_gc = d.astype(jnp.float32).reshape(S5_GROUPS, S5_GROUP_CH)
    y = ut * d_gc
    y_c = uct * d_gc if need_ctx else None
    for dn in range(2):
        fl = (lambda z: jnp.flip(z, axis=0)) if dn == 1 else (lambda z: z)
        xc_re, xc_im = s5_scan(fl(uct), ab_re[dn], ab_im[dn], bb_re[dn], bb_im[dn], None, None)
        x_re, x_im = s5_scan(fl(ut), ab_re[dn], ab_im[dn], bb_re[dn], bb_im[dn], xc_re[-1], xc_im[-1])
        y = y + fl(s5_readout(x_re, x_im, c_re[dn], c_im[dn]))
        if need_ctx:
            y_c = y_c + fl(s5_readout(xc_re, xc_im, c_re[dn], c_im[dn]))
    out = s5_glu(y.transpose(1, 0, 2, 3).reshape(b, n, S5_WIDTH), w_glu).astype(u.dtype)
    out_c = None
    if need_ctx:
        out_c = s5_glu(y_c.transpose(1, 0, 2, 3).reshape(b, l, S5_WIDTH), w_glu).astype(u.dtype)
    return out, out_c


def token_mixing(h, hc, w_in, na_q_gain, na_k_gain, na_rpb, ret_decay, ret_gn,
                 s5_a_re, s5_a_im, s5_log_dt, s5_b_re, s5_b_im, s5_c_re, s5_c_im, s5_d, s5_w_glu,
                 need_ctx):
    rows = h.shape[1] // GRID_W
    cuts = [int(v) for v in np.cumsum(IN_SPLITS)[:-1]]
    qa, ka, va, qb, kb, vb, gb, ub = jnp.split(h @ w_in, cuts, axis=-1)
    qa_c, ka_c, va_c, qb_c, kb_c, vb_c, gb_c, ub_c = jnp.split(hc @ w_in, cuts, axis=-1)

    qk_norm = lambda z, g: rms_norm(to_heads(z, NA_HEADS)) * g
    k_ac = qk_norm(ka_c, na_k_gain)
    v_ac = to_heads(va_c, NA_HEADS)
    y_a = neighborhood_attention(qk_norm(qa, na_q_gain), qk_norm(ka, na_k_gain),
                                 to_heads(va, NA_HEADS), k_ac, v_ac, na_rpb, rows)

    k_scale = RET_QK_DIM ** -0.5
    q_b = axial_rope(to_heads(qb, RET_HEADS))
    k_b = axial_rope(to_heads(kb, RET_HEADS)) * k_scale
    y_b, y_b_c = retention_mixer(q_b, k_b, to_heads(vb, RET_HEADS), gb,
                                 to_heads(qb_c, RET_HEADS), to_heads(kb_c, RET_HEADS) * k_scale,
                                 to_heads(vb_c, RET_HEADS), gb_c, ret_decay, ret_gn, need_ctx)

    y_c, y_c_c = s5_mixer(ub, ub_c, s5_a_re, s5_a_im, s5_log_dt, s5_b_re, s5_b_im,
                          s5_c_re, s5_c_im, s5_d, s5_w_glu, need_ctx)

    y = jnp.concatenate([y_a.astype(h.dtype), y_b.astype(h.dtype), y_c.astype(h.dtype)], axis=-1)
    y_ctx = None
    if need_ctx:
        y_a_c = context_attention(qk_norm(qa_c, na_q_gain), k_ac, v_ac)
        y_ctx = jnp.concatenate([y_a_c.astype(h.dtype), y_b_c.astype(h.dtype), y_c_c.astype(h.dtype)], axis=-1)
    return y, y_ctx


def setup_inputs(seed: int = 0) -> dict:
    key = jax.random.key(seed)
    ks = iter(jax.random.split(key, 32))
    f32 = jnp.float32
    D = D_MODEL

    def nrm(shape, s):
        return jax.random.normal(next(ks), shape, f32) * s

    x = nrm((BATCH, SEQ, D), 1.0)
    c = nrm((BATCH, D), 1.0)
    ctx = nrm((BATCH, CTX_LEN, D), 1.0)
    c_ctx = nrm((D,), 1.0)
    w_mod = nrm((DEPTH, D, N_MOD * D), 0.5 * D ** -0.5)
    b_mod = nrm((DEPTH, N_MOD * D), 0.02)
    ffn1_w_in = nrm((DEPTH, D, 2 * D_FF), D ** -0.5)
    ffn1_w_out = nrm((DEPTH, D_FF, D), D_FF ** -0.5)
    w_in = nrm((DEPTH, D, IN_WIDTH), D ** -0.5)
    w_out = nrm((DEPTH, MIX_WIDTH, D), MIX_WIDTH ** -0.5)
    na_q_gain = 1.0 + nrm((DEPTH, NA_HEAD_DIM), 0.05)
    na_k_gain = 1.0 + nrm((DEPTH, NA_HEAD_DIM), 0.05)
    na_rpb = nrm((DEPTH, NA_HEADS, 2 * NA_ROWS - 1, 2 * NA_COLS - 1), 0.2)
    ret_base = jnp.asarray(np.log(2.0 ** (5 + np.arange(RET_HEADS)) - 1.0), f32)
    ret_decay = ret_base + nrm((DEPTH, 2, RET_HEADS), 0.05)
    ret_gn = 1.0 + nrm((DEPTH, RET_V_WIDTH), 0.05)
    s5_a_re = -0.5 + nrm((DEPTH, 2, S5_GROUPS, S5_STATE), 0.01)
    s5_a_im = jnp.pi * jnp.arange(S5_STATE, dtype=f32) + nrm((DEPTH, 2, S5_GROUPS, S5_STATE), 0.01)
    s5_log_dt = jax.random.uniform(next(ks), (DEPTH, 2, S5_GROUPS), f32,
                                   minval=math.log(1e-3), maxval=math.log(1e-1))
    s5_b_re = nrm((DEPTH, S5_GROUPS, S5_STATE, S5_GROUP_CH), S5_GROUP_CH ** -0.5)
    s5_b_im = nrm((DEPTH, S5_GROUPS, S5_STATE, S5_GROUP_CH), S5_GROUP_CH ** -0.5)
    s5_c_re = nrm((DEPTH, 2, S5_GROUPS, S5_GROUP_CH, S5_STATE), S5_STATE ** -0.5)
    s5_c_im = nrm((DEPTH, 2, S5_GROUPS, S5_GROUP_CH, S5_STATE), S5_STATE ** -0.5)
    s5_d = nrm((DEPTH, S5_WIDTH), 0.5)
    s5_w_glu = nrm((DEPTH, S5_WIDTH, 2 * S5_WIDTH), S5_WIDTH ** -0.5)
    ffn2_w_in = nrm((DEPTH, D, 2 * D_FF), D ** -0.5)
    ffn2_w_out = nrm((DEPTH, D_FF, D), D_FF ** -0.5)
    return {"x": x, "c": c, "ctx": ctx, "c_ctx": c_ctx, "w_mod": w_mod, "b_mod": b_mod,
            "ffn1_w_in": ffn1_w_in, "ffn1_w_out": ffn1_w_out, "w_in": w_in, "w_out": w_out,
            "na_q_gain": na_q_gain, "na_k_gain": na_k_gain, "na_rpb": na_rpb,
            "ret_decay": ret_decay, "ret_gn": ret_gn,
            "s5_a_re": s5_a_re, "s5_a_im": s5_a_im, "s5_log_dt": s5_log_dt,
            "s5_b_re": s5_b_re, "s5_b_im": s5_b_im, "s5_c_re": s5_c_re, "s5_c_im": s5_c_im,
            "s5_d": s5_d, "s5_w_glu": s5_w_glu, "ffn2_w_in": ffn2_w_in, "ffn2_w_out": ffn2_w_out}


def reference(x, c, ctx, c_ctx, w_mod, b_mod, ffn1_w_in, ffn1_w_out, w_in, w_out,
              na_q_gain, na_k_gain, na_rpb, ret_decay, ret_gn,
              s5_a_re, s5_a_im, s5_log_dt, s5_b_re, s5_b_im, s5_c_re, s5_c_im, s5_d, s5_w_glu,
              ffn2_w_in, ffn2_w_out):
    xc = ctx
    sc = jax.nn.silu(c)[:, None, :]
    scc = jax.nn.silu(c_ctx)[None, None, :]
    for l in range(DEPTH):
        need_ctx = l < DEPTH - 1
        m = jnp.split(sc @ w_mod[l] + b_mod[l], N_MOD, axis=-1)
        mc = jnp.split(scc @ w_mod[l] + b_mod[l], N_MOD, axis=-1)
        x = x + 0.5 * m[2] * swiglu(modulate(rms_norm(x), m[0], m[1]), ffn1_w_in[l], ffn1_w_out[l])
        xc = xc + 0.5 * mc[2] * swiglu(modulate(rms_norm(xc), mc[0], mc[1]), ffn1_w_in[l], ffn1_w_out[l])
        y, y_ctx = token_mixing(modulate(rms_norm(x), m[3], m[4]), modulate(rms_norm(xc), mc[3], mc[4]),
                                w_in[l], na_q_gain[l], na_k_gain[l], na_rpb[l], ret_decay[l], ret_gn[l],
                                s5_a_re[l], s5_a_im[l], s5_log_dt[l], s5_b_re[l], s5_b_im[l],
                                s5_c_re[l], s5_c_im[l], s5_d[l], s5_w_glu[l], need_ctx)
        x = x + m[5] * (y @ w_out[l])
        x = x + 0.5 * m[8] * swiglu(modulate(rms_norm(x), m[6], m[7]), ffn2_w_in[l], ffn2_w_out[l])
        if need_ctx:
            xc = xc + mc[5] * (y_ctx @ w_out[l])
            xc = xc + 0.5 * mc[8] * swiglu(modulate(rms_norm(xc), mc[6], mc[7]), ffn2_w_in[l], ffn2_w_out[l])
    return x
```

```python
import functools
import math

import numpy as np
import jax
import jax.numpy as jnp
from jax import lax
from jax.experimental import pallas as pl
from jax.experimental.pallas import tpu as pltpu

F32 = jnp.float32
MXU_DTYPE = jnp.bfloat16

GRID_W = 64
N_MOD = 9
NORM_EPS = 1e-6
NA_HEADS, NA_HEAD_DIM, NA_ROWS, NA_COLS = 8, 32, 8, 16
RET_HEADS, RET_QK_DIM, RET_V_DIM = 4, 64, 128
ROPE_BASE = 10000.0
S5_GROUPS, S5_GROUP_CH, S5_STATE = 16, 16, 64
NA_WIDTH = NA_HEADS * NA_HEAD_DIM
RET_QK_WIDTH = RET_HEADS * RET_QK_DIM
RET_V_WIDTH = RET_HEADS * RET_V_DIM
S5_WIDTH = S5_GROUPS * S5_GROUP_CH

TOKEN_TILE = 512
FFN_CHUNK = 256
NA_BLOCK_ROWS = 4
NA_BLOCK = NA_BLOCK_ROWS * GRID_W
RET_CHUNK = 256
S5_CHUNK = 32
MASK_VALUE = -1e30
VMEM_LIMIT = 56 * 1024 * 1024


def _mxu(a, b):
    return jnp.dot(a.astype(MXU_DTYPE), b.astype(MXU_DTYPE), preferred_element_type=F32)


def _mxu_nt(a, b):
    return lax.dot_general(a.astype(MXU_DTYPE), b.astype(MXU_DTYPE), (((1,), (1,)), ((), ())),
                           preferred_element_type=F32)


def _mxu_tn(a, b):
    return lax.dot_general(a.astype(MXU_DTYPE), b.astype(MXU_DTYPE), (((0,), (0,)), ((), ())),
                           preferred_element_type=F32)


def _silu(x):
    return x * jax.nn.sigmoid(x)


def _cparams(*sem):
    return pltpu.CompilerParams(dimension_semantics=sem, vmem_limit_bytes=VMEM_LIMIT)


def _mod_kernel(s_ref, w_ref, b_ref, o_ref):
    s = _silu(s_ref[...])
    o_ref[...] = _mxu(s, w_ref[...]) + b_ref[...]


def _modulation(cond_rows, w_mod, b_mod):
    depth, d, nd = w_mod.shape
    tn = 1024
    return pl.pallas_call(
        _mod_kernel,
        grid=(depth, nd // tn),
        in_specs=[pl.BlockSpec((8, d), lambda l, j: (0, 0)),
                  pl.BlockSpec((None, d, tn), lambda l, j: (l, 0, j)),
                  pl.BlockSpec((None, 1, tn), lambda l, j: (l, 0, j))],
        out_specs=pl.BlockSpec((None, 8, tn), lambda l, j: (l, 0, j)),
        out_shape=jax.ShapeDtypeStruct((depth, 8, nd), F32),
        compiler_params=_cparams("arbitrary", "arbitrary"),
        name="modulation",
    )(cond_rows, w_mod, b_mod.reshape(depth, 1, nd))


def _mod_rows(mod_ref, row, j, d):
    return mod_ref[pl.ds(row, 1), j * d:(j + 1) * d]


def _norm_mod(x, shift, scale):
    ms = jnp.mean(x * x, axis=-1, keepdims=True)
    return (x * lax.rsqrt(ms + NORM_EPS)) * (1.0 + scale) + shift


def _ffn_kernel(x_ref, mod_ref, win_ref, wout_ref, o_ref, *, j0, tiles_per_batch, dff):
    d = x_ref.shape[1]
    row = pl.program_id(0) // tiles_per_batch
    x = x_ref[...]
    h = _norm_mod(x, _mod_rows(mod_ref, row, j0, d), _mod_rows(mod_ref, row, j0 + 1, d)).astype(MXU_DTYPE)
    acc = jnp.zeros(x.shape, F32)
    for c in range(dff // FFN_CHUNK):
        lo = c * FFN_CHUNK
        a = _mxu(h, win_ref[:, lo:lo + FFN_CHUNK])
        b = _mxu(h, win_ref[:, dff + lo:dff + lo + FFN_CHUNK])
        acc = acc + _mxu(_silu(a) * b, wout_ref[lo:lo + FFN_CHUNK, :])
    o_ref[...] = x + (0.5 * _mod_rows(mod_ref, row, j0 + 2, d)) * acc


def _ffn(x_all, mods_l, w_in, w_out, layer, j0, tiles_per_batch):
    t_all, d = x_all.shape
    dff = w_out.shape[1]
    kern = functools.partial(_ffn_kernel, j0=j0, tiles_per_batch=tiles_per_batch, dff=dff)
    return pl.pallas_call(
        kern,
        grid=(t_all // TOKEN_TILE,),
        in_specs=[pl.BlockSpec((TOKEN_TILE, d), lambda i: (i, 0)),
                  pl.BlockSpec(mods_l.shape, lambda i: (0, 0)),
                  pl.BlockSpec((None, d, 2 * dff), lambda i: (layer, 0, 0)),
                  pl.BlockSpec((None, dff, d), lambda i: (layer, 0, 0))],
        out_specs=pl.BlockSpec((TOKEN_TILE, d), lambda i: (i, 0)),
        out_shape=jax.ShapeDtypeStruct((t_all, d), F32),
        compiler_params=_cparams("arbitrary"),
        name="ffn",
    )(x_all, mods_l, w_in, w_out)


def _head_rms(z, head_sum, gain):
    zz = z * z
    hi = zz.astype(MXU_DTYPE)
    lo = zz - hi.astype(F32)
    ms = (_mxu(hi, head_sum) + _mxu(lo, head_sum)) * (1.0 / NA_HEAD_DIM)
    return z * lax.rsqrt(ms + NORM_EPS) * gain


def _rotate(z, cos, sin_signed):
    tm = z.shape[0]
    lane = lax.broadcasted_iota(jnp.int32, (tm, 128), 1)
    first_half = (lane % RET_QK_DIM) < (RET_QK_DIM // 2)
    outs = []
    for c in range(z.shape[1] // 128):
        zc = z[:, c * 128:(c + 1) * 128]
        partner = jnp.where(first_half, pltpu.roll(zc, 128 - RET_QK_DIM // 2, 1), pltpu.roll(zc, RET_QK_DIM // 2, 1))
        outs.append(zc * cos[:, c * 128:(c + 1) * 128] + partner * sin_signed[:, c * 128:(c + 1) * 128])
    return jnp.concatenate(outs, axis=1)


def _inproj_kernel(x_ref, mod_ref, w_ref, hs_ref, qg_ref, kg_ref, cos_ref, sin_ref,
                   qa_ref, ka_ref, va_ref, qb_ref, kb_ref, vb_ref, gb_ref, ub_ref, *, tiles_per_batch):
    d = x_ref.shape[1]
    row = pl.program_id(0) // tiles_per_batch
    x = x_ref[...]
    h = _norm_mod(x, _mod_rows(mod_ref, row, 3, d), _mod_rows(mod_ref, row, 4, d)).astype(MXU_DTYPE)

    def proj(lo, width):
        return _mxu(h, w_ref[:, lo:lo + width])

    head_sum = hs_ref[...]
    o = 0
    qa = _head_rms(proj(o, NA_WIDTH), head_sum, qg_ref[...]) * (NA_HEAD_DIM ** -0.5)
    qa_ref[...] = qa.astype(qa_ref.dtype)
    o += NA_WIDTH
    ka_ref[...] = _head_rms(proj(o, NA_WIDTH), head_sum, kg_ref[...]).astype(ka_ref.dtype)
    o += NA_WIDTH
    va_ref[...] = proj(o, NA_WIDTH).astype(va_ref.dtype)
    o += NA_WIDTH
    cos = cos_ref[...]
    sin = sin_ref[...]
    qb_ref[...] = _rotate(proj(o, RET_QK_WIDTH), cos, sin).astype(qb_ref.dtype)
    o += RET_QK_WIDTH
    kb_ref[...] = (_rotate(proj(o, RET_QK_WIDTH), cos, sin) * (RET_QK_DIM ** -0.5)).astype(kb_ref.dtype)
    o += RET_QK_WIDTH
    vb_ref[...] = proj(o, RET_V_WIDTH).astype(vb_ref.dtype)
    o += RET_V_WIDTH
    gb_ref[...] = proj(o, RET_V_WIDTH)
    o += RET_V_WIDTH
    ub_ref[...] = proj(o, S5_WIDTH)


def _inproj(x_all, mods_l, w_in, layer, head_sum, q_gain, k_gain, cos_tab, sin_tab, tiles_per_batch, n_lat_tiles):
    t_all, d = x_all.shape
    in_width = w_in.shape[2]
    tm = TOKEN_TILE
    tok = lambda w: pl.BlockSpec((tm, w), lambda i: (i, 0))
    full = lambda a: pl.BlockSpec(a.shape, lambda i: (0,) * a.ndim)
    tab = pl.BlockSpec((tm, RET_QK_WIDTH),
                       lambda i: (jnp.where(i < n_lat_tiles, i % tiles_per_batch, tiles_per_batch), 0))
    widths = (NA_WIDTH, NA_WIDTH, NA_WIDTH, RET_QK_WIDTH, RET_QK_WIDTH, RET_V_WIDTH, RET_V_WIDTH, S5_WIDTH)
    dtypes = (MXU_DTYPE,) * 6 + (F32, F32)
    return pl.pallas_call(
        functools.partial(_inproj_kernel, tiles_per_batch=tiles_per_batch),
        grid=(t_all // tm,),
        in_specs=[tok(d), full(mods_l), pl.BlockSpec((None, d, in_width), lambda i: (layer, 0, 0)),
                  full(head_sum), full(q_gain), full(k_gain), tab, tab],
        out_specs=[tok(w) for w in widths],
        out_shape=[jax.ShapeDtypeStruct((t_all, w), dt) for w, dt in zip(widths, dtypes)],
        compiler_params=_cparams("arbitrary"),
        name="inproj",
    )(x_all, mods_l, w_in, head_sum, q_gain, k_gain, cos_tab, sin_tab)


def _softmax_heads(q, parts):
    lane = lax.broadcasted_iota(jnp.int32, (1, NA_WIDTH), 1)
    acc = jnp.zeros((q.shape[0], NA_WIDTH), F32)
    for h in range(NA_HEADS):
        in_head = (lane // NA_HEAD_DIM) == h
        scores = []
        for k, _, bias_fn in parts:
            s = _mxu_nt(q, jnp.where(in_head, k, jnp.zeros_like(k)))
            scores.append(s if bias_fn is None else s + bias_fn(h))
        m = functools.reduce(jnp.maximum, [jnp.max(s, axis=-1, keepdims=True) for s in scores])
        ps = [jnp.exp(s - m) for s in scores]
        l = functools.reduce(jnp.add, [jnp.sum(p, axis=-1, keepdims=True) for p in ps])
        o = functools.reduce(jnp.add, [_mxu(p, jnp.where(in_head, v, jnp.zeros_like(v)))
                                       for p, (_, v, _) in zip(ps, parts)])
        acc = acc + o * (1.0 / l)
    return acc


def _na_kernel(q_ref, k0_ref, k1_ref, k2_ref, v0_ref, v1_ref, v2_ref, kc_ref, vc_ref, bias_ref, o_ref, *, nblk):
    i = pl.program_id(1)
    q = q_ref[...]
    kc = kc_ref[...]
    vc = vc_ref[...]

    @pl.when(i < nblk)
    def _():
        k = jnp.concatenate([k0_ref[...], k1_ref[...], k2_ref[...]], axis=0)
        v = jnp.concatenate([v0_ref[...], v1_ref[...], v2_ref[...]], axis=0)
        out = _softmax_heads(q, [(k, v, lambda h: bias_ref[h]), (kc, vc, None)])
        o_ref[...] = out.astype(o_ref.dtype)

    @pl.when(i == nblk)
    def _():
        o_ref[...] = _softmax_heads(q, [(kc, vc, None)]).astype(o_ref.dtype)


def _na_bias_tables(rpb, rows):
    w = np.arange(GRID_W)
    kc = np.arange(GRID_W)
    ci = np.clip(kc[None, :] - w[:, None] + NA_COLS - 1, 0, 2 * NA_COLS - 2)
    col = jnp.take(rpb, jnp.asarray(ci.reshape(-1)), axis=2)
    col = col.reshape(NA_HEADS, 2 * NA_ROWS - 1, GRID_W, GRID_W)
    n_key_rows = 3 * NA_BLOCK_ROWS
    per_a = []
    for a in range(NA_BLOCK_ROWS):
        lo = NA_ROWS - 1 - NA_BLOCK_ROWS - a
        per_a.append(jnp.transpose(col[:, lo:lo + n_key_rows], (0, 2, 1, 3)))
    base = jnp.stack(per_a, axis=1).reshape(NA_HEADS, NA_BLOCK, n_key_rows * GRID_W)
    valid = np.zeros((3, NA_BLOCK, n_key_rows * GRID_W), bool)
    cs = np.clip(w - NA_COLS // 2, 0, GRID_W - NA_COLS)
    for var, r0 in enumerate((0, 2 * NA_BLOCK_ROWS, rows - NA_BLOCK_ROWS)):
        for a in range(NA_BLOCK_ROWS):
            r = r0 + a
            rs = np.clip(r - NA_ROWS // 2, 0, rows - NA_ROWS)
            kr = r0 - NA_BLOCK_ROWS + np.arange(n_key_rows)
            row_ok = (kr >= 0) & (kr < rows) & (kr >= rs) & (kr < rs + NA_ROWS)
            col_ok = (kc[None, :] >= cs[:, None]) & (kc[None, :] < cs[:, None] + NA_COLS)
            ok = row_ok[None, :, None] & col_ok[:, None, :]
            valid[var, a * GRID_W:(a + 1) * GRID_W] = ok.reshape(GRID_W, -1)
    return jnp.where(jnp.asarray(valid)[:, None], base[None], MASK_VALUE).astype(F32)


def _neighborhood_attention(qa, ka, va, bias, batch, n_lat):
    t_all = qa.shape[0]
    nblk = n_lat // NA_BLOCK
    ctx0 = batch * nblk
    blk = (NA_BLOCK, NA_WIDTH)

    def qmap(b, i):
        return (jnp.where(i < nblk, b * nblk + i, ctx0 + b), 0)

    def kmap(off):
        return lambda b, i: (b * nblk + jnp.clip(i + off, 0, nblk - 1), 0)

    cmap = lambda b, i: (ctx0 + b, 0)
    bmap = lambda b, i: (jnp.where(i == 0, 0, jnp.where(i >= nblk - 1, 2, 1)), 0, 0, 0)
    return pl.pallas_call(
        functools.partial(_na_kernel, nblk=nblk),
        grid=(batch, nblk + 1),
        in_specs=[pl.BlockSpec(blk, qmap),
                  pl.BlockSpec(blk, kmap(-1)), pl.BlockSpec(blk, kmap(0)), pl.BlockSpec(blk, kmap(1)),
                  pl.BlockSpec(blk, kmap(-1)), pl.BlockSpec(blk, kmap(0)), pl.BlockSpec(blk, kmap(1)),
                  pl.BlockSpec(blk, cmap), pl.BlockSpec(blk, cmap),
                  pl.BlockSpec((None,) + bias.shape[1:], bmap)],
        out_specs=pl.BlockSpec(blk, qmap),
        out_shape=jax.ShapeDtypeStruct((t_all, NA_WIDTH), MXU_DTYPE),
        compiler_params=_cparams("arbitrary", "arbitrary"),
        name="natten",
    )(qa, ka, ka, ka, va, va, va, ka, va, bias)


def _ret_state_update(state_ref, k, v, w_end, g_chunk, block_mask):
    kw = k.astype(F32) * w_end
    state_ref[...] = g_chunk * state_ref[...] + _mxu_tn(kw, v) * block_mask


def _ret_fwd_kernel(q_ref, k_ref, v_ref, dmat_ref, win_ref, wend_ref, gch_ref, bm_ref, o_ref, state_ref):
    @pl.when(pl.program_id(1) == 0)
    def _():
        state_ref[...] = jnp.zeros_like(state_ref)

    q = q_ref[...]
    k = k_ref[...]
    v = v_ref[...]
    cross = _mxu(q, state_ref[...]) * win_ref[...]
    lane = lax.broadcasted_iota(jnp.int32, (1, RET_QK_WIDTH), 1)
    for h in range(RET_HEADS):
        qh = jnp.where((lane // RET_QK_DIM) == h, q, jnp.zeros_like(q))
        a = _mxu_nt(qh, k) * dmat_ref[h]
        sl = slice(h * RET_V_DIM, (h + 1) * RET_V_DIM)
        o_ref[:, sl] = _mxu(a, v[:, sl]) + cross[:, sl]
    _ret_state_update(state_ref, k, v, wend_ref[...], gch_ref[...], bm_ref[...])


def _ret_bwd_kernel(q_ref, k_ref, v_ref, of_ref, gate_ref, win_ref, wend_ref, gch_ref, bm_ref, gn_ref,
                    y_ref, state_ref):
    @pl.when(pl.program_id(1) == 0)
    def _():
        state_ref[...] = jnp.zeros_like(state_ref)

    q = q_ref[...]
    o = of_ref[...] + _mxu(q, state_ref[...]) * win_ref[...]
    gate = _silu(gate_ref[...])
    gain = gn_ref[...]
    for h in range(RET_HEADS):
        sl = slice(h * RET_V_DIM, (h + 1) * RET_V_DIM)
        oh = o[:, sl]
        mu = jnp.mean(oh, axis=-1, keepdims=True)
        cen = oh - mu
        var = jnp.mean(cen * cen, axis=-1, keepdims=True)
        y_ref[:, sl] = (gate[:, sl] * (cen * lax.rsqrt(var + NORM_EPS) * gain[:, sl])).astype(y_ref.dtype)
    _ret_state_update(state_ref, k_ref[...], v_ref[...], wend_ref[...], gch_ref[...], bm_ref[...])


def _ret_tables(decay):
    c = RET_CHUNK
    log_g = jax.nn.log_sigmoid(decay.astype(F32))
    pos = jnp.arange(c, dtype=F32)
    diff = pos[:, None] - pos[None, :]
    lgf = log_g[0][:, None, None]
    lgb = log_g[1][:, None, None]
    dmat = jnp.where(diff >= 0, jnp.exp(lgf * jnp.maximum(diff, 0.0)), jnp.exp(lgb * jnp.maximum(-diff, 0.0)))
    rep = lambda a, w: jnp.repeat(a, w, axis=-1)
    win_f = rep(jnp.exp(log_g[0][None, :] * (pos[:, None] + 1.0)), RET_V_DIM)
    wend_f = rep(jnp.exp(log_g[0][None, :] * (c - 1.0 - pos[:, None])), RET_QK_DIM)
    win_b = rep(jnp.exp(log_g[1][None, :] * (c - pos[:, None])), RET_V_DIM)
    wend_b = rep(jnp.exp(log_g[1][None, :] * pos[:, None]), RET_QK_DIM)
    gch_f = rep(jnp.exp(log_g[0] * c)[None, :], RET_V_DIM)
    gch_b = rep(jnp.exp(log_g[1] * c)[None, :], RET_V_DIM)
    return dmat, (win_f, wend_f, gch_f), (win_b, wend_b, gch_b)


def _retention(qb, kb, vb, gb, decay, gn_gain, batch, n_lat):
    t_all = qb.shape[0]
    c = RET_CHUNK
    nch = n_lat // c
    ctx0 = batch * nch
    dmat, (win_f, wend_f, gch_f), (win_b, wend_b, gch_b) = _ret_tables(decay)
    bm = (np.arange(RET_QK_WIDTH)[:, None] // RET_QK_DIM == np.arange(RET_V_WIDTH)[None, :] // RET_V_DIM)
    bm = jnp.asarray(bm, F32)
    fmap = lambda b, i: (jnp.where(i == 0, ctx0 + b, b * nch + i - 1), 0)
    rmap = lambda b, i: (jnp.where(i == 0, ctx0 + b, b * nch + nch - i), 0)
    full = lambda a: pl.BlockSpec(a.shape, lambda b, i: (0,) * a.ndim)
    spec = lambda w, m: pl.BlockSpec((c, w), m)
    state = pltpu.VMEM((RET_QK_WIDTH, RET_V_WIDTH), F32)
    o_f = pl.pallas_call(
        _ret_fwd_kernel,
        grid=(batch, nch + 1),
        in_specs=[spec(RET_QK_WIDTH, fmap), spec(RET_QK_WIDTH, fmap), spec(RET_V_WIDTH, fmap),
                  full(dmat), full(win_f), full(wend_f), full(gch_f), full(bm)],
        out_specs=spec(RET_V_WIDTH, fmap),
        out_shape=jax.ShapeDtypeStruct((t_all, RET_V_WIDTH), F32),
        scratch_shapes=[state],
        compiler_params=_cparams("arbitrary", "arbitrary"),
        name="retention_fwd",
    )(qb, kb, vb, dmat, win_f, wend_f, gch_f, bm)
    gn = gn_gain.reshape(1, RET_V_WIDTH).astype(F32)
    return pl.pallas_call(
        _ret_bwd_kernel,
        grid=(batch, nch + 1),
        in_specs=[spec(RET_QK_WIDTH, rmap), spec(RET_QK_WIDTH, rmap), spec(RET_V_WIDTH, rmap),
                  spec(RET_V_WIDTH, rmap), spec(RET_V_WIDTH, rmap),
                  full(win_b), full(wend_b), full(gch_b), full(bm), full(gn)],
        out_specs=spec(RET_V_WIDTH, rmap),
        out_shape=jax.ShapeDtypeStruct((t_all, RET_V_WIDTH), MXU_DTYPE),
        scratch_shapes=[state],
        compiler_params=_cparams("arbitrary", "arbitrary"),
        name="retention_bwd",
    )(qb, kb, vb, o_f, gb, win_b, wend_b, gch_b, bm, gn)


def _s5_discretize(a_re, a_im, log_dt, b_re, b_im):
    a_re = jnp.minimum(a_re.astype(F32), -1e-4)
    a_im = a_im.astype(F32)
    dt = jnp.exp(log_dt.astype(F32))[..., None]
    mag = jnp.exp(dt * a_re)
    ab_re = mag * jnp.cos(dt * a_im)
    ab_im = mag * jnp.sin(dt * a_im)
    den = a_re * a_re + a_im * a_im
    nr = ab_re - 1.0
    f_re = ((nr * a_re + ab_im * a_im) / den)[..., None]
    f_im = ((ab_im * a_re - nr * a_im) / den)[..., None]
    br = b_re.astype(F32)[None]
    bi = b_im.astype(F32)[None]
    return ab_re, ab_im, f_re * br - f_im * bi, f_re * bi + f_im * br


def _complex_powers(ar, ai, n):
    pr = jnp.ones((1,) + ar.shape, F32)
    pi = jnp.zeros((1,) + ar.shape, F32)
    cr, ci = ar, ai
    while pr.shape[0] < n:
        pr, pi = (jnp.concatenate([pr, pr * cr - pi * ci], axis=0),
                  jnp.concatenate([pi, pr * ci + pi * cr], axis=0))
        cr, ci = cr * cr - ci * ci, 2.0 * cr * ci
    return pr[:n], pi[:n]


def _s5_kgen_kernel(b_ref, ca_ref, o_ref):
    o_ref[...] = jnp.dot(b_ref[...], ca_ref[...], preferred_element_type=F32, precision=lax.Precision.HIGHEST)


def _toeplitz_lower(k):
    L = k.shape[-1]
    v = jnp.concatenate([k, jnp.zeros_like(k)], axis=-1)
    x = jnp.tile(v, (1,) * (k.ndim - 1) + (L,))[..., :L * (2 * L - 1)]
    return x.reshape(k.shape[:-1] + (L, 2 * L - 1))[..., :L]


def _s5_operators(a_re, a_im, log_dt, b_re, b_im, c_re, c_im):
    G, P, Cg, L = S5_GROUPS, S5_STATE, S5_GROUP_CH, S5_CHUNK
    ab_re, ab_im, bb_re, bb_im = _s5_discretize(a_re, a_im, log_dt, b_re, b_im)
    pr, pi = _complex_powers(ab_re, ab_im, L + 1)
    cr = c_re.astype(F32)
    ci = c_im.astype(F32)
    car = jnp.einsum('dngp,ngcp->ngpdc', pr[:L], cr) - jnp.einsum('dngp,ngcp->ngpdc', pi[:L], ci)
    cai = jnp.einsum('dngp,ngcp->ngpdc', pr[:L], ci) + jnp.einsum('dngp,ngcp->ngpdc', pi[:L], cr)
    ca = jnp.concatenate([car, -cai], axis=2).reshape(2, G, 2 * P, L * Cg)
    bt = jnp.concatenate([jnp.swapaxes(bb_re, 2, 3), jnp.swapaxes(bb_im, 2, 3)], axis=3)
    kgen = pl.pallas_call(
        _s5_kgen_kernel,
        grid=(2, G),
        in_specs=[pl.BlockSpec((None, None, Cg, 2 * P), lambda n, g: (n, g, 0, 0)),
                  pl.BlockSpec((None, None, 2 * P, L * Cg), lambda n, g: (n, g, 0, 0))],
        out_specs=pl.BlockSpec((None, None, Cg, L * Cg), lambda n, g: (n, g, 0, 0)),
        out_shape=jax.ShapeDtypeStruct((2, G, Cg, L * Cg), F32),
        compiler_params=_cparams("arbitrary", "arbitrary"),
        name="s5_kernel_gen",
    )(bt, ca)
    kk = jnp.transpose(kgen.reshape(2, G, Cg, L, Cg), (0, 1, 2, 4, 3))
    tf = _toeplitz_lower(kk[0])
    tb = jnp.swapaxes(_toeplitz_lower(kk[1]), -1, -2)
    conv = jnp.transpose(tf + tb, (0, 3, 1, 4, 2)).reshape(G, L * Cg, L * Cg)

    def state_in(n, exps):
        wr = jnp.einsum('sgp,gpc->gscp', pr[exps, n], bb_re[n]) - jnp.einsum('sgp,gpc->gscp', pi[exps, n], bb_im[n])
        wi = jnp.einsum('sgp,gpc->gscp', pr[exps, n], bb_im[n]) + jnp.einsum('sgp,gpc->gscp', pi[exps, n], bb_re[n])
        return (jnp.concatenate([wr, wi], axis=-1).reshape(G, L * Cg, 2 * P),
                jnp.concatenate([wi, wr], axis=-1).reshape(G, L * Cg, 2 * P))

    def state_out(n, exps):
        er = jnp.einsum('tgp,gcp->gptc', pr[exps, n], cr[n]) - jnp.einsum('tgp,gcp->gptc', pi[exps, n], ci[n])
        ei = jnp.einsum('tgp,gcp->gptc', pr[exps, n], ci[n]) + jnp.einsum('tgp,gcp->gptc', pi[exps, n], cr[n])
        return jnp.concatenate([er, -ei], axis=1).reshape(G, 2 * P, L * Cg)

    s = np.arange(L)
    ff, ffs = state_in(0, L - 1 - s)
    fb, fbs = state_in(1, s)
    f_cat = jnp.concatenate([ff, ffs, fb, fbs], axis=-1)
    e_f = state_out(0, s + 1)
    e_b = state_out(1, L - s)
    return conv, f_cat, e_f, e_b, (pr[L], pi[L])


def _s5_scan_mults(alr, ali, levels):
    rows = []
    cr, ci = alr, ali
    for _ in range(levels):
        rows.append(jnp.stack([jnp.concatenate([cr, cr], -1), jnp.concatenate([-ci, ci], -1),
                               jnp.concatenate([ci, -ci], -1)], axis=-2))
        cr, ci = cr * cr - ci * ci, 2.0 * cr * ci
    return jnp.stack(rows, axis=2)


def _shift_rows(x, k, down):
    n = x.shape[0]
    row = lax.broadcasted_iota(jnp.int32, (n, 1), 0)
    if down:
        return jnp.where(row >= k, pltpu.roll(x, k, 0), 0.0)
    return jnp.where(row < n - k, pltpu.roll(x, n - k, 0), 0.0)


def _s5_scan(loc, locs, mult_ref, init, inits, forward):
    n = loc.shape[0]
    a1, a2, a2s = mult_ref[0, 0:1], mult_ref[0, 1:2], mult_ref[0, 2:3]
    first = (lax.broadcasted_iota(jnp.int32, (n, 1), 0) == (0 if forward else n - 1)).astype(F32)
    x = loc + first * (a1 * init + a2 * inits)
    xs = locs + first * (a1 * inits + a2s * init)
    k, lvl = 1, 0
    while k < n:
        a1, a2, a2s = mult_ref[lvl, 0:1], mult_ref[lvl, 1:2], mult_ref[lvl, 2:3]
        sx, sxs = _shift_rows(x, k, forward), _shift_rows(xs, k, forward)
        x, xs = x + a1 * sx + a2 * sxs, xs + a1 * sxs + a2s * sx
        k, lvl = 2 * k, lvl + 1
    last = slice(n - 1, n) if forward else slice(0, 1)
    prev = _shift_rows(x, 1, forward) + first * init
    return prev, x[last], xs[last]


def _s5_kernel(u_ref, conv_ref, fcat_ref, ef_ref, eb_ref, mf_ref, mb_ref, d_ref, y_ref, *, batch, n_lat_ch, n_ctx_ch):
    P2 = 2 * S5_STATE
    u = u_ref[...]
    ub = u.astype(MXU_DTYPE)
    y_ref[...] = _mxu(ub, conv_ref[...]) + u * d_ref[...]
    loc = _mxu(ub, fcat_ref[...])
    zero = jnp.zeros((1, P2), F32)
    for b in range(batch):
        ctx = slice(batch * n_lat_ch + b * n_ctx_ch, batch * n_lat_ch + (b + 1) * n_ctx_ch)
        lat = slice(b * n_lat_ch, (b + 1) * n_lat_ch)
        for (c0, e_ref, m_ref, fwd) in ((0, ef_ref, mf_ref, True), (2 * P2, eb_ref, mb_ref, False)):
            pc, xl, xls = _s5_scan(loc[ctx, c0:c0 + P2], loc[ctx, c0 + P2:c0 + 2 * P2], m_ref, zero, zero, fwd)
            pl_, _, _ = _s5_scan(loc[lat, c0:c0 + P2], loc[lat, c0 + P2:c0 + 2 * P2], m_ref, xl, xls, fwd)
            y_ref[ctx, :] += _mxu(pc, e_ref[...])
            y_ref[lat, :] += _mxu(pl_, e_ref[...])


def _s5_mixer(ub, ops, d_skip, batch, n_lat, n_ctx):
    conv, f_cat, e_f, e_b, (alr, ali) = ops
    G, Cg, L, P2 = S5_GROUPS, S5_GROUP_CH, S5_CHUNK, 2 * S5_STATE
    t_all = ub.shape[0]
    rows = t_all // L
    n_lat_ch, n_ctx_ch = n_lat // L, n_ctx // L
    levels = max(1, int(math.ceil(math.log2(max(n_lat_ch, n_ctx_ch)))))
    mults = _s5_scan_mults(alr, ali, levels)
    ug = jnp.transpose(ub.reshape(rows, L, G, Cg), (2, 0, 1, 3)).reshape(G, rows, L * Cg)
    d_t = jnp.tile(d_skip.astype(F32).reshape(G, 1, Cg), (1, 1, L))
    per_g = lambda *s: pl.BlockSpec((None,) + s, lambda g: (g,) + (0,) * len(s))
    mspec = lambda n: pl.BlockSpec((None, None, levels, 3, P2), lambda g: (n, g, 0, 0, 0))
    y = pl.pallas_call(
        functools.partial(_s5_kernel, batch=batch, n_lat_ch=n_lat_ch, n_ctx_ch=n_ctx_ch),
        grid=(G,),
        in_specs=[per_g(rows, L * Cg), per_g(L * Cg, L * Cg), per_g(L * Cg, 4 * P2),
                  per_g(P2, L * Cg), per_g(P2, L * Cg), mspec(0), mspec(1), per_g(1, L * Cg)],
        out_specs=per_g(rows, L * Cg),
        out_shape=jax.ShapeDtypeStruct((G, rows, L * Cg), F32),
        compiler_params=_cparams("arbitrary"),
        name="s5_conv",
    )(ug, conv.astype(MXU_DTYPE), f_cat.astype(MXU_DTYPE), e_f.astype(MXU_DTYPE), e_b.astype(MXU_DTYPE),
      mults, mults, d_t)
    return jnp.transpose(y.reshape(G, rows, L, Cg), (1, 2, 0, 3)).reshape(t_all, G * Cg)


def _outproj_kernel(x_ref, mod_ref, ya_ref, yb_ref, yc_ref, wglu_ref, wo_ref, o_ref, *, tiles_per_batch):
    d = x_ref.shape[1]
    row = pl.program_id(0) // tiles_per_batch
    glu = _mxu(jax.nn.gelu(yc_ref[...]), wglu_ref[...])
    s5 = glu[:, :S5_WIDTH] * jax.nn.sigmoid(glu[:, S5_WIDTH:])
    mix = (_mxu(ya_ref[...], wo_ref[0:NA_WIDTH, :])
           + _mxu(yb_ref[...], wo_ref[NA_WIDTH:NA_WIDTH + RET_V_WIDTH, :])
           + _mxu(s5, wo_ref[NA_WIDTH + RET_V_WIDTH:, :]))
    o_ref[...] = x_ref[...] + _mod_rows(mod_ref, row, 5, d) * mix


def _outproj(x_all, mods_l, ya, yb, yc, w_glu, w_out, layer, tiles_per_batch):
    t_all, d = x_all.shape
    tm = TOKEN_TILE
    tok = lambda w: pl.BlockSpec((tm, w), lambda i: (i, 0))
    return pl.pallas_call(
        functools.partial(_outproj_kernel, tiles_per_batch=tiles_per_batch),
        grid=(t_all // tm,),
        in_specs=[tok(d), pl.BlockSpec(mods_l.shape, lambda i: (0, 0)), tok(NA_WIDTH), tok(RET_V_WIDTH), tok(S5_WIDTH),
                  pl.BlockSpec((None,) + w_glu.shape[1:], lambda i: (layer, 0, 0)),
                  pl.BlockSpec((None,) + w_out.shape[1:], lambda i: (layer, 0, 0))],
        out_specs=tok(d),
        out_shape=jax.ShapeDtypeStruct((t_all, d), F32),
        compiler_params=_cparams("arbitrary"),
        name="outproj",
    )(x_all, mods_l, ya, yb, yc, w_glu, w_out)


def _rope_tables(n_lat, n_ctx):
    nf = RET_QK_DIM // 4
    inv = ROPE_BASE ** (-jnp.arange(nf, dtype=F32) / nf)
    t = jnp.arange(n_lat)
    row = (t // GRID_W).astype(F32)
    col = (t % GRID_W).astype(F32)
    ang = jnp.concatenate([row[:, None] * inv, col[:, None] * inv], axis=-1)
    cos, sin = jnp.cos(ang), jnp.sin(ang)
    cos_l = jnp.tile(jnp.concatenate([cos, cos], axis=-1), (1, RET_HEADS))
    sin_l = jnp.tile(jnp.concatenate([-sin, sin], axis=-1), (1, RET_HEADS))
    pad = TOKEN_TILE
    return (jnp.concatenate([cos_l, jnp.ones((pad, RET_QK_WIDTH), F32)], axis=0),
            jnp.concatenate([sin_l, jnp.zeros((pad, RET_QK_WIDTH), F32)], axis=0))


def kernel(x, c, ctx, c_ctx, w_mod, b_mod, ffn1_w_in, ffn1_w_out, w_in, w_out, na_q_gain, na_k_gain, na_rpb,
           ret_decay, ret_gn, s5_a_re, s5_a_im, s5_log_dt, s5_b_re, s5_b_im, s5_c_re, s5_c_im, s5_d, s5_w_glu,
           ffn2_w_in, ffn2_w_out):
    batch, n_lat, d = x.shape
    n_ctx = ctx.shape[1]
    depth = w_mod.shape[0]
    assert n_ctx == NA_BLOCK == RET_CHUNK and batch * n_ctx == TOKEN_TILE
    assert n_lat % TOKEN_TILE == 0 and n_lat // NA_BLOCK >= 4
    assert batch + 1 <= 8
    tiles_per_batch = n_lat // TOKEN_TILE
    n_lat_tiles = batch * tiles_per_batch
    rows = n_lat // GRID_W

    bf = lambda a: a.astype(MXU_DTYPE)
    ffn1_w_in, ffn1_w_out, ffn2_w_in, ffn2_w_out = bf(ffn1_w_in), bf(ffn1_w_out), bf(ffn2_w_in), bf(ffn2_w_out)
    w_in, w_out, s5_w_glu = bf(w_in), bf(w_out), bf(s5_w_glu)

    cond = jnp.concatenate([c, c_ctx[None, :], jnp.zeros((8 - batch - 1, d), F32)], axis=0)
    mods = _modulation(cond, w_mod, b_mod)

    head_sum = jnp.asarray(np.arange(NA_WIDTH)[:, None] // NA_HEAD_DIM == np.arange(NA_WIDTH)[None, :] // NA_HEAD_DIM,
                           MXU_DTYPE)
    cos_tab, sin_tab = _rope_tables(n_lat, n_ctx)
    x_all = jnp.concatenate([x.reshape(batch * n_lat, d), ctx.reshape(batch * n_ctx, d)], axis=0)

    for l in range(depth):
        mods_l = mods[l]
        x_all = _ffn(x_all, mods_l, ffn1_w_in, ffn1_w_out, l, 0, tiles_per_batch)
        qa, ka, va, qb, kb, vb, gb, ub = _inproj(
            x_all, mods_l, w_in, l, head_sum,
            jnp.tile(na_q_gain[l].astype(F32), NA_HEADS)[None, :], jnp.tile(na_k_gain[l].astype(F32), NA_HEADS)[None, :],
            cos_tab, sin_tab, tiles_per_batch, n_lat_tiles)
        ya = _neighborhood_attention(qa, ka, va, _na_bias_tables(na_rpb[l].astype(F32), rows), batch, n_lat)
        yb = _retention(qb, kb, vb, gb, ret_decay[l], ret_gn[l], batch, n_lat)
        ops = _s5_operators(s5_a_re[l], s5_a_im[l], s5_log_dt[l], s5_b_re[l], s5_b_im[l], s5_c_re[l], s5_c_im[l])
        yc = _s5_mixer(ub, ops, s5_d[l], batch, n_lat, n_ctx)
        x_all = _outproj(x_all, mods_l, ya, yb, yc, s5_w_glu, w_out, l, tiles_per_batch)
        x_all = _ffn(x_all, mods_l, ffn2_w_in, ffn2_w_out, l, 6, tiles_per_batch)
    return x_all[:batch * n_lat].reshape(batch, n_lat, d)
```

```python
import functools
import math

import numpy as np
import jax
import jax.numpy as jnp
from jax import lax
from jax.experimental import pallas as pl
from jax.experimental.pallas import tpu as pltpu

F32 = jnp.float32
MXU_DTYPE = jnp.bfloat16

GRID_W = 64
N_MOD = 9
NORM_EPS = 1e-6
NA_HEADS, NA_HEAD_DIM, NA_ROWS, NA_COLS = 8, 32, 8, 16
RET_HEADS, RET_QK_DIM, RET_V_DIM = 4, 64, 128
ROPE_BASE = 10000.0
S5_GROUPS, S5_GROUP_CH, S5_STATE = 16, 16, 64
NA_WIDTH = NA_HEADS * NA_HEAD_DIM
RET_QK_WIDTH = RET_HEADS * RET_QK_DIM
RET_V_WIDTH = RET_HEADS * RET_V_DIM
S5_WIDTH = S5_GROUPS * S5_GROUP_CH

TOKEN_TILE = 512
FFN_CHUNK = 256
NA_BLOCK_ROWS = 4
NA_BLOCK = NA_BLOCK_ROWS * GRID_W
RET_CHUNK = 256
S5_CHUNK = 32
MASK_VALUE = -1e30
LANES = 128
VMEM_LIMIT = 56 * 1024 * 1024


def _mxu(a, b):
    return jnp.dot(a.astype(MXU_DTYPE), b.astype(MXU_DTYPE), preferred_element_type=F32)


def _mxu_nt(a, b):
    return lax.dot_general(a.astype(MXU_DTYPE), b.astype(MXU_DTYPE), (((1,), (1,)), ((), ())),
                           preferred_element_type=F32)


def _mxu_tn(a, b):
    return lax.dot_general(a.astype(MXU_DTYPE), b.astype(MXU_DTYPE), (((0,), (0,)), ((), ())),
                           preferred_element_type=F32)


def _silu(x):
    return x * jax.nn.sigmoid(x)


def _cparams(*sem):
    return pltpu.CompilerParams(dimension_semantics=sem, vmem_limit_bytes=VMEM_LIMIT)


def _mod_kernel(s_ref, w_ref, b_ref, o_ref):
    s = _silu(s_ref[...])
    o_ref[...] = _mxu(s, w_ref[...]) + b_ref[...]


def _modulation(cond_rows, w_mod, b_mod):
    depth, d, nd = w_mod.shape
    tn = 1024
    return pl.pallas_call(
        _mod_kernel,
        grid=(depth, nd // tn),
        in_specs=[pl.BlockSpec((8, d), lambda l, j: (0, 0)),
                  pl.BlockSpec((None, d, tn), lambda l, j: (l, 0, j)),
                  pl.BlockSpec((None, 1, tn), lambda l, j: (l, 0, j))],
        out_specs=pl.BlockSpec((None, 8, tn), lambda l, j: (l, 0, j)),
        out_shape=jax.ShapeDtypeStruct((depth, 8, nd), F32),
        compiler_params=_cparams("arbitrary", "arbitrary"),
        name="modulation",
    )(cond_rows, w_mod, b_mod.reshape(depth, 1, nd))


def _mod_rows(mod_ref, row, j, d):
    return mod_ref[pl.ds(row, 1), j * d:(j + 1) * d]


def _norm_mod(x, shift, scale):
    ms = jnp.mean(x * x, axis=-1, keepdims=True)
    return (x * lax.rsqrt(ms + NORM_EPS)) * (1.0 + scale) + shift


def _ffn_body(x, mod_ref, row, j0, win_ref, wout_ref):
    d = x.shape[1]
    dff = wout_ref.shape[0]
    h = _norm_mod(x, _mod_rows(mod_ref, row, j0, d), _mod_rows(mod_ref, row, j0 + 1, d)).astype(MXU_DTYPE)
    acc = jnp.zeros(x.shape, F32)
    for c in range(dff // FFN_CHUNK):
        lo = c * FFN_CHUNK
        a = _mxu(h, win_ref[:, lo:lo + FFN_CHUNK])
        b = _mxu(h, win_ref[:, dff + lo:dff + lo + FFN_CHUNK])
        acc = acc + _mxu(_silu(a) * b, wout_ref[lo:lo + FFN_CHUNK, :])
    return x + (0.5 * _mod_rows(mod_ref, row, j0 + 2, d)) * acc


def _ffn_kernel(x_ref, mod_ref, win_ref, wout_ref, o_ref, *, tiles_per_batch):
    row = pl.program_id(0) // tiles_per_batch
    o_ref[...] = _ffn_body(x_ref[...], mod_ref, row, 0, win_ref, wout_ref)


def _ffn_split_kernel(x_ref, ctx_ref, mod_ref, win_ref, wout_ref, o_ref, *, tiles_per_batch, n_lat_tiles):
    i = pl.program_id(0)
    x = jnp.where(i < n_lat_tiles, x_ref[...], ctx_ref[...])
    o_ref[...] = _ffn_body(x, mod_ref, i // tiles_per_batch, 0, win_ref, wout_ref)


def _resident(shape, layer):
    return pl.BlockSpec((None,) + tuple(shape[1:]), lambda i: (layer,) + (0,) * (len(shape) - 1),
                        pipeline_mode=pl.Buffered(1))


def _ffn_first(x_lat, x_ctx, mods_l, w_in, w_out, layer, tiles_per_batch):
    d = x_lat.shape[1]
    tm = TOKEN_TILE
    tok = pl.BlockSpec((tm, d), lambda i: (i, 0))
    mod = pl.BlockSpec(mods_l.shape, lambda i: (0, 0))
    weights = [_resident(w_in.shape, layer), _resident(w_out.shape, layer)]
    if x_ctx is None:
        t_all = x_lat.shape[0]
        kern = functools.partial(_ffn_kernel, tiles_per_batch=tiles_per_batch)
        in_specs, args = [tok, mod] + weights, (x_lat, mods_l, w_in, w_out)
    else:
        n_lat_tiles = x_lat.shape[0] // tm
        t_all = x_lat.shape[0] + x_ctx.shape[0]
        kern = functools.partial(_ffn_split_kernel, tiles_per_batch=tiles_per_batch, n_lat_tiles=n_lat_tiles)
        in_specs = [pl.BlockSpec((tm, d), lambda i: (jnp.minimum(i, n_lat_tiles - 1), 0)),
                    pl.BlockSpec((tm, d), lambda i: (0, 0)), mod] + weights
        args = (x_lat, x_ctx, mods_l, w_in, w_out)
    return pl.pallas_call(
        kern, grid=(t_all // tm,), in_specs=in_specs, out_specs=tok,
        out_shape=jax.ShapeDtypeStruct((t_all, d), F32),
        compiler_params=_cparams("arbitrary"), name="ffn1",
    )(*args)


def _merge_blocks(srcs, shifts):
    nb = LANES // S5_GROUP_CH
    lane_blk = lax.broadcasted_iota(jnp.int32, srcs[0].shape, 1) // S5_GROUP_CH
    acc = None
    for k, (src, sh) in enumerate(zip(srcs, shifts)):
        piece = src if sh % nb == 0 else pltpu.roll(src, (sh % nb) * S5_GROUP_CH, 1)
        acc = piece if acc is None else jnp.where(lane_blk == k, piece, acc)
    return acc


def _token_scratch(tm):
    return pltpu.VMEM((S5_WIDTH // LANES, tm, LANES), F32)


def _tokens_to_groups(tok_ref, ug_ref):
    L, nb = S5_CHUNK, LANES // S5_GROUP_CH
    n = tok_ref.shape[1] // L
    for hb in range(tok_ref.shape[0]):
        for q in range(L // nb):
            srcs = [tok_ref[hb, pl.ds(q * nb + r, n, stride=L), :] for r in range(nb)]
            for j in range(nb):
                ug_ref[hb * nb + j, :, q * LANES:(q + 1) * LANES] = _merge_blocks(srcs, [r - j for r in range(nb)])


def _groups_to_tokens(yg_ref, tok_ref):
    L, nb = S5_CHUNK, LANES // S5_GROUP_CH
    n = tok_ref.shape[1] // L
    for hb in range(tok_ref.shape[0]):
        for q in range(L // nb):
            srcs = [yg_ref[hb * nb + j, :, q * LANES:(q + 1) * LANES] for j in range(nb)]
            for r in range(nb):
                tok_ref[hb, pl.ds(q * nb + r, n, stride=L), :] = _merge_blocks(srcs, [j - r for j in range(nb)])


def _head_rms(z, head_sum, gain):
    zz = z * z
    hi = zz.astype(MXU_DTYPE)
    lo = zz - hi.astype(F32)
    ms = (_mxu(hi, head_sum) + _mxu(lo, head_sum)) * (1.0 / NA_HEAD_DIM)
    return z * lax.rsqrt(ms + NORM_EPS) * gain


def _rotate(z, cos, sin_signed):
    tm = z.shape[0]
    lane = lax.broadcasted_iota(jnp.int32, (tm, 128), 1)
    first_half = (lane % RET_QK_DIM) < (RET_QK_DIM // 2)
    outs = []
    for c in range(z.shape[1] // 128):
        zc = z[:, c * 128:(c + 1) * 128]
        partner = jnp.where(first_half, pltpu.roll(zc, 128 - RET_QK_DIM // 2, 1), pltpu.roll(zc, RET_QK_DIM // 2, 1))
        outs.append(zc * cos[:, c * 128:(c + 1) * 128] + partner * sin_signed[:, c * 128:(c + 1) * 128])
    return jnp.concatenate(outs, axis=1)


def _inproj_kernel(x_ref, mod_ref, w_ref, hs_ref, qg_ref, kg_ref, cos_ref, sin_ref,
                   qa_ref, ka_ref, va_ref, qb_ref, kb_ref, vb_ref, gb_ref, ug_ref, tok_ref, *, tiles_per_batch):
    d = x_ref.shape[1]
    row = pl.program_id(0) // tiles_per_batch
    x = x_ref[...]
    h = _norm_mod(x, _mod_rows(mod_ref, row, 3, d), _mod_rows(mod_ref, row, 4, d)).astype(MXU_DTYPE)

    def proj(lo, width):
        return _mxu(h, w_ref[:, lo:lo + width])

    head_sum = hs_ref[...]
    o = 0
    qa = _head_rms(proj(o, NA_WIDTH), head_sum, qg_ref[...]) * (NA_HEAD_DIM ** -0.5)
    qa_ref[...] = qa.astype(qa_ref.dtype)
    o += NA_WIDTH
    ka_ref[...] = _head_rms(proj(o, NA_WIDTH), head_sum, kg_ref[...]).astype(ka_ref.dtype)
    o += NA_WIDTH
    va_ref[...] = proj(o, NA_WIDTH).astype(va_ref.dtype)
    o += NA_WIDTH
    cos = cos_ref[...]
    sin = sin_ref[...]
    qb_ref[...] = _rotate(proj(o, RET_QK_WIDTH), cos, sin).astype(qb_ref.dtype)
    o += RET_QK_WIDTH
    kb_ref[...] = (_rotate(proj(o, RET_QK_WIDTH), cos, sin) * (RET_QK_DIM ** -0.5)).astype(kb_ref.dtype)
    o += RET_QK_WIDTH
    vb_ref[...] = proj(o, RET_V_WIDTH).astype(vb_ref.dtype)
    o += RET_V_WIDTH
    gb_ref[...] = proj(o, RET_V_WIDTH)
    o += RET_V_WIDTH
    u = proj(o, S5_WIDTH)
    for hb in range(S5_WIDTH // LANES):
        tok_ref[hb] = u[:, hb * LANES:(hb + 1) * LANES]
    _tokens_to_groups(tok_ref, ug_ref)


def _inproj(x_all, mods_l, w_in, layer, head_sum, q_gain, k_gain, cos_tab, sin_tab, tiles_per_batch, n_lat_tiles):
    t_all, d = x_all.shape
    in_width = w_in.shape[2]
    tm = TOKEN_TILE
    tok = lambda w: pl.BlockSpec((tm, w), lambda i: (i, 0))
    full = lambda a: pl.BlockSpec(a.shape, lambda i: (0,) * a.ndim)
    tab = pl.BlockSpec((tm, RET_QK_WIDTH),
                       lambda i: (jnp.where(i < n_lat_tiles, i % tiles_per_batch, tiles_per_batch), 0))
    widths = (NA_WIDTH, NA_WIDTH, NA_WIDTH, RET_QK_WIDTH, RET_QK_WIDTH, RET_V_WIDTH, RET_V_WIDTH)
    dtypes = (MXU_DTYPE,) * 6 + (F32,)
    chunk_w = S5_CHUNK * S5_GROUP_CH
    return pl.pallas_call(
        functools.partial(_inproj_kernel, tiles_per_batch=tiles_per_batch),
        grid=(t_all // tm,),
        in_specs=[tok(d), full(mods_l), _resident(w_in.shape, layer),
                  full(head_sum), full(q_gain), full(k_gain), tab, tab],
        out_specs=[tok(w) for w in widths] + [pl.BlockSpec((S5_GROUPS, tm // S5_CHUNK, chunk_w), lambda i: (0, i, 0))],
        out_shape=[jax.ShapeDtypeStruct((t_all, w), dt) for w, dt in zip(widths, dtypes)]
        + [jax.ShapeDtypeStruct((S5_GROUPS, t_all // S5_CHUNK, chunk_w), F32)],
        scratch_shapes=[_token_scratch(tm)],
        compiler_params=_cparams("arbitrary"),
        name="inproj",
    )(x_all, mods_l, w_in, head_sum, q_gain, k_gain, cos_tab, sin_tab)


def _softmax_heads(q, parts):
    lane = lax.broadcasted_iota(jnp.int32, (1, NA_WIDTH), 1)
    acc = jnp.zeros((q.shape[0], NA_WIDTH), F32)
    for h in range(NA_HEADS):
        in_head = (lane // NA_HEAD_DIM) == h
        qh = jnp.where(in_head, q, jnp.zeros_like(q))
        scores = []
        for k, _, bias_fn in parts:
            s = _mxu_nt(qh, k)
            scores.append(s if bias_fn is None else s + bias_fn(h))
        m = functools.reduce(jnp.maximum, [jnp.max(s, axis=-1, keepdims=True) for s in scores])
        ps = [jnp.exp(s - m) for s in scores]
        l = functools.reduce(jnp.add, [jnp.sum(p, axis=-1, keepdims=True) for p in ps])
        o = functools.reduce(jnp.add, [_mxu(p, v) for p, (_, v, _) in zip(ps, parts)])
        acc = acc + o * jnp.where(in_head, 1.0 / l, 0.0)
    return acc


def _na_kernel(q_ref, k0_ref, k1_ref, k2_ref, v0_ref, v1_ref, v2_ref, kc_ref, vc_ref, bias_ref, o_ref, *, nblk):
    i = pl.program_id(1)
    q = q_ref[...]
    kc = kc_ref[...]
    vc = vc_ref[...]

    @pl.when(i < nblk)
    def _():
        k = jnp.concatenate([k0_ref[...], k1_ref[...], k2_ref[...]], axis=0)
        v = jnp.concatenate([v0_ref[...], v1_ref[...], v2_ref[...]], axis=0)
        out = _softmax_heads(q, [(k, v, lambda h: bias_ref[h]), (kc, vc, None)])
        o_ref[...] = out.astype(o_ref.dtype)

    @pl.when(i == nblk)
    def _():
        o_ref[...] = _softmax_heads(q, [(kc, vc, None)]).astype(o_ref.dtype)


def _na_bias_tables(rpb, rows):
    w = np.arange(GRID_W)
    kc = np.arange(GRID_W)
    ci = np.clip(kc[None, :] - w[:, None] + NA_COLS - 1, 0, 2 * NA_COLS - 2)
    col = jnp.take(rpb, jnp.asarray(ci.reshape(-1)), axis=2)
    col = col.reshape(NA_HEADS, 2 * NA_ROWS - 1, GRID_W, GRID_W)
    n_key_rows = 3 * NA_BLOCK_ROWS
    per_a = []
    for a in range(NA_BLOCK_ROWS):
        lo = NA_ROWS - 1 - NA_BLOCK_ROWS - a
        per_a.append(jnp.transpose(col[:, lo:lo + n_key_rows], (0, 2, 1, 3)))
    base = jnp.stack(per_a, axis=1).reshape(NA_HEADS, NA_BLOCK, n_key_rows * GRID_W)
    valid = np.zeros((3, NA_BLOCK, n_key_rows * GRID_W), bool)
    cs = np.clip(w - NA_COLS // 2, 0, GRID_W - NA_COLS)
    for var, r0 in enumerate((0, 2 * NA_BLOCK_ROWS, rows - NA_BLOCK_ROWS)):
        for a in range(NA_BLOCK_ROWS):
            r = r0 + a
            rs = np.clip(r - NA_ROWS // 2, 0, rows - NA_ROWS)
            kr = r0 - NA_BLOCK_ROWS + np.arange(n_key_rows)
            row_ok = (kr >= 0) & (kr < rows) & (kr >= rs) & (kr < rs + NA_ROWS)
            col_ok = (kc[None, :] >= cs[:, None]) & (kc[None, :] < cs[:, None] + NA_COLS)
            ok = row_ok[None, :, None] & col_ok[:, None, :]
            valid[var, a * GRID_W:(a + 1) * GRID_W] = ok.reshape(GRID_W, -1)
    return jnp.where(jnp.asarray(valid)[:, None], base[None], MASK_VALUE).astype(F32)


def _neighborhood_attention(qa, ka, va, bias, batch, n_lat):
    t_all = qa.shape[0]
    nblk = n_lat // NA_BLOCK
    ctx0 = batch * nblk
    blk = (NA_BLOCK, NA_WIDTH)

    def qmap(b, i):
        return (jnp.where(i < nblk, b * nblk + i, ctx0 + b), 0)

    def kmap(off):
        return lambda b, i: (b * nblk + jnp.clip(i + off, 0, nblk - 1), 0)

    cmap = lambda b, i: (ctx0 + b, 0)
    bmap = lambda b, i: (jnp.where(i == 0, 0, jnp.where(i >= nblk - 1, 2, 1)), 0, 0, 0)
    return pl.pallas_call(
        functools.partial(_na_kernel, nblk=nblk),
        grid=(batch, nblk + 1),
        in_specs=[pl.BlockSpec(blk, qmap),
                  pl.BlockSpec(blk, kmap(-1)), pl.BlockSpec(blk, kmap(0)), pl.BlockSpec(blk, kmap(1)),
                  pl.BlockSpec(blk, kmap(-1)), pl.BlockSpec(blk, kmap(0)), pl.BlockSpec(blk, kmap(1)),
                  pl.BlockSpec(blk, cmap), pl.BlockSpec(blk, cmap),
                  pl.BlockSpec((None,) + bias.shape[1:], bmap)],
        out_specs=pl.BlockSpec(blk, qmap),
        out_shape=jax.ShapeDtypeStruct((t_all, NA_WIDTH), MXU_DTYPE),
        compiler_params=_cparams("arbitrary", "arbitrary"),
        name="natten",
    )(qa, ka, ka, ka, va, va, va, ka, va, bias)


def _ret_state_update(state_ref, k, v, w_end, g_chunk, block_mask):
    kw = k.astype(F32) * w_end
    state_ref[...] = g_chunk * state_ref[...] + _mxu_tn(kw, v) * block_mask


def _ret_fwd_kernel(q_ref, k_ref, v_ref, dmat_ref, win_ref, wend_ref, gch_ref, bm_ref, o_ref, state_ref):
    @pl.when(pl.program_id(1) == 0)
    def _():
        state_ref[...] = jnp.zeros_like(state_ref)

    q = q_ref[...]
    k = k_ref[...]
    v = v_ref[...]
    cross = _mxu(q, state_ref[...]) * win_ref[...]
    lane = lax.broadcasted_iota(jnp.int32, (1, RET_QK_WIDTH), 1)
    for h in range(RET_HEADS):
        qh = jnp.where((lane // RET_QK_DIM) == h, q, jnp.zeros_like(q))
        a = _mxu_nt(qh, k) * dmat_ref[h]
        sl = slice(h * RET_V_DIM, (h + 1) * RET_V_DIM)
        o_ref[:, sl] = _mxu(a, v[:, sl]) + cross[:, sl]
    _ret_state_update(state_ref, k, v, wend_ref[...], gch_ref[...], bm_ref[...])


def _ret_bwd_kernel(q_ref, k_ref, v_ref, of_ref, gate_ref, win_ref, wend_ref, gch_ref, bm_ref, gn_ref,
                    y_ref, state_ref):
    @pl.when(pl.program_id(1) == 0)
    def _():
        state_ref[...] = jnp.zeros_like(state_ref)

    q = q_ref[...]
    o = of_ref[...] + _mxu(q, state_ref[...]) * win_ref[...]
    gate = _silu(gate_ref[...])
    gain = gn_ref[...]
    for h in range(RET_HEADS):
        sl = slice(h * RET_V_DIM, (h + 1) * RET_V_DIM)
        oh = o[:, sl]
        mu = jnp.mean(oh, axis=-1, keepdims=True)
        cen = oh - mu
        var = jnp.mean(cen * cen, axis=-1, keepdims=True)
        y_ref[:, sl] = (gate[:, sl] * (cen * lax.rsqrt(var + NORM_EPS) * gain[:, sl])).astype(y_ref.dtype)
    _ret_state_update(state_ref, k_ref[...], v_ref[...], wend_ref[...], gch_ref[...], bm_ref[...])


def _ret_tables(decay):
    c = RET_CHUNK
    log_g = jax.nn.log_sigmoid(decay.astype(F32))
    pos = jnp.arange(c, dtype=F32)
    diff = pos[:, None] - pos[None, :]
    lgf = log_g[0][:, None, None]
    lgb = log_g[1][:, None, None]
    dmat = jnp.where(diff >= 0, jnp.exp(lgf * jnp.maximum(diff, 0.0)), jnp.exp(lgb * jnp.maximum(-diff, 0.0)))
    rep = lambda a, w: jnp.repeat(a, w, axis=-1)
    win_f = rep(jnp.exp(log_g[0][None, :] * (pos[:, None] + 1.0)), RET_V_DIM)
    wend_f = rep(jnp.exp(log_g[0][None, :] * (c - 1.0 - pos[:, None])), RET_QK_DIM)
    win_b = rep(jnp.exp(log_g[1][None, :] * (c - pos[:, None])), RET_V_DIM)
    wend_b = rep(jnp.exp(log_g[1][None, :] * pos[:, None]), RET_QK_DIM)
    gch_f = rep(jnp.exp(log_g[0] * c)[None, :], RET_V_DIM)
    gch_b = rep(jnp.exp(log_g[1] * c)[None, :], RET_V_DIM)
    return dmat, (win_f, wend_f, gch_f), (win_b, wend_b, gch_b)


def _retention(qb, kb, vb, gb, decay, gn_gain, batch, n_lat):
    t_all = qb.shape[0]
    c = RET_CHUNK
    nch = n_lat // c
    ctx0 = batch * nch
    dmat, (win_f, wend_f, gch_f), (win_b, wend_b, gch_b) = _ret_tables(decay)
    bm = (np.arange(RET_QK_WIDTH)[:, None] // RET_QK_DIM == np.arange(RET_V_WIDTH)[None, :] // RET_V_DIM)
    bm = jnp.asarray(bm, F32)
    fmap = lambda b, i: (jnp.where(i == 0, ctx0 + b, b * nch + i - 1), 0)
    rmap = lambda b, i: (jnp.where(i == 0, ctx0 + b, b * nch + nch - i), 0)
    full = lambda a: pl.BlockSpec(a.shape, lambda b, i: (0,) * a.ndim)
    spec = lambda w, m: pl.BlockSpec((c, w), m)
    state = pltpu.VMEM((RET_QK_WIDTH, RET_V_WIDTH), F32)
    o_f = pl.pallas_call(
        _ret_fwd_kernel,
        grid=(batch, nch + 1),
        in_specs=[spec(RET_QK_WIDTH, fmap), spec(RET_QK_WIDTH, fmap), spec(RET_V_WIDTH, fmap),
                  full(dmat), full(win_f), full(wend_f), full(gch_f), full(bm)],
        out_specs=spec(RET_V_WIDTH, fmap),
        out_shape=jax.ShapeDtypeStruct((t_all, RET_V_WIDTH), F32),
        scratch_shapes=[state],
        compiler_params=_cparams("arbitrary", "arbitrary"),
        name="retention_fwd",
    )(qb, kb, vb, dmat, win_f, wend_f, gch_f, bm)
    gn = gn_gain.reshape(1, RET_V_WIDTH).astype(F32)
    return pl.pallas_call(
        _ret_bwd_kernel,
        grid=(batch, nch + 1),
        in_specs=[spec(RET_QK_WIDTH, rmap), spec(RET_QK_WIDTH, rmap), spec(RET_V_WIDTH, rmap),
                  spec(RET_V_WIDTH, rmap), spec(RET_V_WIDTH, rmap),
                  full(win_b), full(wend_b), full(gch_b), full(bm), full(gn)],
        out_specs=spec(RET_V_WIDTH, rmap),
        out_shape=jax.ShapeDtypeStruct((t_all, RET_V_WIDTH), MXU_DTYPE),
        scratch_shapes=[state],
        compiler_params=_cparams("arbitrary", "arbitrary"),
        name="retention_bwd",
    )(qb, kb, vb, o_f, gb, win_b, wend_b, gch_b, bm, gn)


def _s5_discretize(a_re, a_im, log_dt, b_re, b_im):
    a_re = jnp.minimum(a_re.astype(F32), -1e-4)
    a_im = a_im.astype(F32)
    dt = jnp.exp(log_dt.astype(F32))[..., None]
    mag = jnp.exp(dt * a_re)
    ab_re = mag * jnp.cos(dt * a_im)
    ab_im = mag * jnp.sin(dt * a_im)
    den = a_re * a_re + a_im * a_im
    nr = ab_re - 1.0
    f_re = ((nr * a_re + ab_im * a_im) / den)[..., None]
    f_im = ((ab_im * a_re - nr * a_im) / den)[..., None]
    br = b_re.astype(F32)[None]
    bi = b_im.astype(F32)[None]
    return ab_re, ab_im, f_re * br - f_im * bi, f_re * bi + f_im * br


def _complex_powers(ar, ai, n):
    pr = jnp.ones((1,) + ar.shape, F32)
    pi = jnp.zeros((1,) + ar.shape, F32)
    cr, ci = ar, ai
    while pr.shape[0] < n:
        pr, pi = (jnp.concatenate([pr, pr * cr - pi * ci], axis=0),
                  jnp.concatenate([pi, pr * ci + pi * cr], axis=0))
        cr, ci = cr * cr - ci * ci, 2.0 * cr * ci
    return pr[:n], pi[:n]


def _s5_operators(a_re, a_im, log_dt, b_re, b_im, c_re, c_im, levels):
    G, P, Cg, L = S5_GROUPS, S5_STATE, S5_GROUP_CH, S5_CHUNK
    ab_re, ab_im, bb_re, bb_im = _s5_discretize(a_re, a_im, log_dt, b_re, b_im)
    pr, pi = _complex_powers(ab_re, ab_im, L + 1)
    cr = c_re.astype(F32)
    ci = c_im.astype(F32)
    bt = jnp.concatenate([jnp.swapaxes(bb_re, 2, 3), jnp.swapaxes(bb_im, 2, 3)], axis=3)

    def state_in(n, exps):
        wr = jnp.einsum('sgp,gpc->gscp', pr[exps, n], bb_re[n]) - jnp.einsum('sgp,gpc->gscp', pi[exps, n], bb_im[n])
        wi = jnp.einsum('sgp,gpc->gscp', pr[exps, n], bb_im[n]) + jnp.einsum('sgp,gpc->gscp', pi[exps, n], bb_re[n])
        return (jnp.concatenate([wr, wi], axis=-1).reshape(G, L * Cg, 2 * P),
                jnp.concatenate([wi, wr], axis=-1).reshape(G, L * Cg, 2 * P))

    def state_out(n, exps):
        er = jnp.einsum('tgp,gcp->gptc', pr[exps, n], cr[n]) - jnp.einsum('tgp,gcp->gptc', pi[exps, n], ci[n])
        ei = jnp.einsum('tgp,gcp->gptc', pr[exps, n], ci[n]) + jnp.einsum('tgp,gcp->gptc', pi[exps, n], cr[n])
        return jnp.concatenate([er, -ei], axis=1).reshape(G, 2 * P, L * Cg)

    s = np.arange(L)
    ca = jnp.stack([state_out(0, s), state_out(1, L - 1 - s)])
    ff, ffs = state_in(0, L - 1 - s)
    fb, fbs = state_in(1, s)
    f = jnp.concatenate([ff, ffs, fb, fbs], axis=-1).astype(MXU_DTYPE)
    e = jnp.stack([state_out(0, s + 1), state_out(1, L - s)]).astype(MXU_DTYPE)
    return bt, ca, f, e, _s5_scan_mults(pr[L], pi[L], levels)


def _s5_scan_mults(alr, ali, levels):
    rows = []
    cr, ci = alr, ali
    for _ in range(levels):
        rows.append(jnp.stack([jnp.concatenate([cr, cr], -1), jnp.concatenate([-ci, ci], -1),
                               jnp.concatenate([ci, -ci], -1)], axis=-2))
        cr, ci = cr * cr - ci * ci, 2.0 * cr * ci
    return jnp.stack(rows, axis=2)


def _shift_rows(x, k, down):
    n = x.shape[0]
    row = lax.broadcasted_iota(jnp.int32, (n, 1), 0)
    if down:
        return jnp.where(row >= k, pltpu.roll(x, k, 0), 0.0)
    return jnp.where(row < n - k, pltpu.roll(x, n - k, 0), 0.0)


def _s5_scan(loc, locs, mult_ref, init, inits, forward):
    n = loc.shape[0]
    a1, a2, a2s = mult_ref[0, 0:1], mult_ref[0, 1:2], mult_ref[0, 2:3]
    first = (lax.broadcasted_iota(jnp.int32, (n, 1), 0) == (0 if forward else n - 1)).astype(F32)
    x = loc + first * (a1 * init + a2 * inits)
    xs = locs + first * (a1 * inits + a2s * init)
    k, lvl = 1, 0
    while k < n:
        a1, a2, a2s = mult_ref[lvl, 0:1], mult_ref[lvl, 1:2], mult_ref[lvl, 2:3]
        sx, sxs = _shift_rows(x, k, forward), _shift_rows(xs, k, forward)
        x, xs = x + a1 * sx + a2 * sxs, xs + a1 * sxs + a2s * sx
        k, lvl = 2 * k, lvl + 1
    last = slice(n - 1, n) if forward else slice(0, 1)
    prev = _shift_rows(x, 1, forward) + first * init
    return prev, x[last], xs[last]


def _s5_toeplitz(bt_ref, ca_ref, conv_ref):
    L, Cg = S5_CHUNK, S5_GROUP_CH
    width = L * Cg
    hi = lax.Precision.HIGHEST
    kf = jnp.dot(bt_ref[0], ca_ref[0], preferred_element_type=F32, precision=hi)
    kb = jnp.dot(bt_ref[1], ca_ref[1], preferred_element_type=F32, precision=hi)
    lane = lax.broadcasted_iota(jnp.int32, (Cg, width), 1)
    for s in range(L):
        right = s * Cg
        left = (L - 1 - s) * Cg
        fwd = kf if right == 0 else jnp.where(lane >= right, pltpu.roll(kf, right, 1), 0.0)
        bwd = kb if left == 0 else jnp.where(lane < width - left, pltpu.roll(kb, width - left, 1), 0.0)
        conv_ref[s * Cg:(s + 1) * Cg, :] = fwd + bwd


def _s5_kernel(u_ref, bt_ref, ca_ref, fcat_ref, e_ref, mult_ref, d_ref, y_ref, conv_ref, *, batch, n_lat_ch, n_ctx_ch):
    P2 = 2 * S5_STATE
    _s5_toeplitz(bt_ref, ca_ref, conv_ref)
    u = u_ref[...]
    ub = u.astype(MXU_DTYPE)
    y_ref[...] = _mxu(ub, conv_ref[...]) + u * d_ref[...]
    loc = _mxu(ub, fcat_ref[...])
    zero = jnp.zeros((1, P2), F32)
    for b in range(batch):
        ctx = slice(batch * n_lat_ch + b * n_ctx_ch, batch * n_lat_ch + (b + 1) * n_ctx_ch)
        lat = slice(b * n_lat_ch, (b + 1) * n_lat_ch)
        for n, fwd in ((0, True), (1, False)):
            c0 = n * 2 * P2
            m_ref = mult_ref.at[n]
            pc, xl, xls = _s5_scan(loc[ctx, c0:c0 + P2], loc[ctx, c0 + P2:c0 + 2 * P2], m_ref, zero, zero, fwd)
            pl_, _, _ = _s5_scan(loc[lat, c0:c0 + P2], loc[lat, c0 + P2:c0 + 2 * P2], m_ref, xl, xls, fwd)
            y_ref[ctx, :] += _mxu(pc, e_ref[n])
            y_ref[lat, :] += _mxu(pl_, e_ref[n])


def _s5_mixer(ug, ops, d_skip, layer, batch, n_lat, n_ctx):
    bt, ca, f_cat, e, mults = ops
    G, Cg, L, P2 = S5_GROUPS, S5_GROUP_CH, S5_CHUNK, 2 * S5_STATE
    rows, width = ug.shape[1], L * Cg
    levels = mults.shape[3]
    d_t = jnp.tile(d_skip.astype(F32).reshape(G, 1, Cg), (1, 1, L))
    per_g = lambda *s: pl.BlockSpec((None,) + s, lambda g: (g,) + (0,) * len(s))
    per_dir = lambda *s: pl.BlockSpec((None, 2, None) + s, lambda g: (layer, 0, g) + (0,) * len(s))
    return pl.pallas_call(
        functools.partial(_s5_kernel, batch=batch, n_lat_ch=n_lat // L, n_ctx_ch=n_ctx // L),
        grid=(G,),
        in_specs=[per_g(rows, width), per_dir(Cg, P2), per_dir(P2, width),
                  pl.BlockSpec((None, None, width, 4 * P2), lambda g: (layer, g, 0, 0)),
                  per_dir(P2, width), per_dir(levels, 3, P2), per_g(1, width)],
        out_specs=per_g(rows, width),
        out_shape=jax.ShapeDtypeStruct((G, rows, width), F32),
        scratch_shapes=[pltpu.VMEM((width, width), F32)],
        compiler_params=_cparams("arbitrary"),
        name="s5_conv",
    )(ug, bt, ca, f_cat, e, mults, d_t)


def _mix_ffn_kernel(x_ref, mod_ref, ya_ref, yb_ref, yg_ref, wglu_ref, wo_ref, win_ref, wout_ref, o_ref, tok_ref,
                    *, tiles_per_batch):
    d = x_ref.shape[1]
    row = pl.program_id(0) // tiles_per_batch
    _groups_to_tokens(yg_ref, tok_ref)
    yc = jnp.concatenate([tok_ref[hb] for hb in range(S5_WIDTH // LANES)], axis=1)
    glu = _mxu(jax.nn.gelu(yc), wglu_ref[...])
    s5 = glu[:, :S5_WIDTH] * jax.nn.sigmoid(glu[:, S5_WIDTH:])
    mix = (_mxu(ya_ref[...], wo_ref[0:NA_WIDTH, :])
           + _mxu(yb_ref[...], wo_ref[NA_WIDTH:NA_WIDTH + RET_V_WIDTH, :])
           + _mxu(s5, wo_ref[NA_WIDTH + RET_V_WIDTH:, :]))
    x = x_ref[...] + _mod_rows(mod_ref, row, 5, d) * mix
    o_ref[...] = _ffn_body(x, mod_ref, row, 6, win_ref, wout_ref)


def _mix_ffn(x_all, mods_l, ya, yb, yg, w_glu, w_out, w_in2, w_out2, layer, tiles_per_batch, n_tiles):
    d = x_all.shape[1]
    tm = TOKEN_TILE
    tok = lambda w: pl.BlockSpec((tm, w), lambda i: (i, 0))
    return pl.pallas_call(
        functools.partial(_mix_ffn_kernel, tiles_per_batch=tiles_per_batch),
        grid=(n_tiles,),
        in_specs=[tok(d), pl.BlockSpec(mods_l.shape, lambda i: (0, 0)), tok(NA_WIDTH), tok(RET_V_WIDTH),
                  pl.BlockSpec((S5_GROUPS, tm // S5_CHUNK, yg.shape[2]), lambda i: (0, i, 0)),
                  _resident(w_glu.shape, layer), _resident(w_out.shape, layer),
                  _resident(w_in2.shape, layer), _resident(w_out2.shape, layer)],
        out_specs=tok(d),
        out_shape=jax.ShapeDtypeStruct((n_tiles * tm, d), F32),
        scratch_shapes=[_token_scratch(tm)],
        compiler_params=_cparams("arbitrary"),
        name="mix_ffn2",
    )(x_all, mods_l, ya, yb, yg, w_glu, w_out, w_in2, w_out2)


def _rope_tables(n_lat, n_ctx):
    nf = RET_QK_DIM // 4
    inv = ROPE_BASE ** (-jnp.arange(nf, dtype=F32) / nf)
    t = jnp.arange(n_lat)
    row = (t // GRID_W).astype(F32)
    col = (t % GRID_W).astype(F32)
    ang = jnp.concatenate([row[:, None] * inv, col[:, None] * inv], axis=-1)
    cos, sin = jnp.cos(ang), jnp.sin(ang)
    cos_l = jnp.tile(jnp.concatenate([cos, cos], axis=-1), (1, RET_HEADS))
    sin_l = jnp.tile(jnp.concatenate([-sin, sin], axis=-1), (1, RET_HEADS))
    pad = TOKEN_TILE
    return (jnp.concatenate([cos_l, jnp.ones((pad, RET_QK_WIDTH), F32)], axis=0),
            jnp.concatenate([sin_l, jnp.zeros((pad, RET_QK_WIDTH), F32)], axis=0))


def kernel(x, c, ctx, c_ctx, w_mod, b_mod, ffn1_w_in, ffn1_w_out, w_in, w_out, na_q_gain, na_k_gain, na_rpb,
           ret_decay, ret_gn, s5_a_re, s5_a_im, s5_log_dt, s5_b_re, s5_b_im, s5_c_re, s5_c_im, s5_d, s5_w_glu,
           ffn2_w_in, ffn2_w_out):
    batch, n_lat, d = x.shape
    n_ctx = ctx.shape[1]
    depth = w_mod.shape[0]
    assert n_ctx == NA_BLOCK == RET_CHUNK and batch * n_ctx == TOKEN_TILE
    assert n_lat % TOKEN_TILE == 0 and n_lat // NA_BLOCK >= 4
    assert batch + 1 <= 8
    tiles_per_batch = n_lat // TOKEN_TILE
    n_lat_tiles = batch * tiles_per_batch
    rows = n_lat // GRID_W

    bf = lambda a: a.astype(MXU_DTYPE)
    ffn1_w_in, ffn1_w_out, ffn2_w_in, ffn2_w_out = bf(ffn1_w_in), bf(ffn1_w_out), bf(ffn2_w_in), bf(ffn2_w_out)
    w_in, w_out, s5_w_glu = bf(w_in), bf(w_out), bf(s5_w_glu)

    cond = jnp.concatenate([c, c_ctx[None, :], jnp.zeros((8 - batch - 1, d), F32)], axis=0)
    mods = _modulation(cond, w_mod, b_mod)

    head_sum = jnp.asarray(np.arange(NA_WIDTH)[:, None] // NA_HEAD_DIM == np.arange(NA_WIDTH)[None, :] // NA_HEAD_DIM,
                           MXU_DTYPE)
    cos_tab, sin_tab = _rope_tables(n_lat, n_ctx)
    levels = max(1, int(math.ceil(math.log2(max(n_lat, n_ctx) // S5_CHUNK))))
    s5_ops = jax.vmap(functools.partial(_s5_operators, levels=levels))(
        s5_a_re, s5_a_im, s5_log_dt, s5_b_re, s5_b_im, s5_c_re, s5_c_im)

    x_all = x.reshape(batch * n_lat, d)
    x_ctx = ctx.reshape(batch * n_ctx, d)
    for l in range(depth):
        mods_l = mods[l]
        x_all = _ffn_first(x_all, x_ctx if l == 0 else None, mods_l, ffn1_w_in, ffn1_w_out, l, tiles_per_batch)
        qa, ka, va, qb, kb, vb, gb, ug = _inproj(
            x_all, mods_l, w_in, l, head_sum,
            jnp.tile(na_q_gain[l].astype(F32), NA_HEADS)[None, :], jnp.tile(na_k_gain[l].astype(F32), NA_HEADS)[None, :],
            cos_tab, sin_tab, tiles_per_batch, n_lat_tiles)
        ya = _neighborhood_attention(qa, ka, va, _na_bias_tables(na_rpb[l].astype(F32), rows), batch, n_lat)
        yb = _retention(qb, kb, vb, gb, ret_decay[l], ret_gn[l], batch, n_lat)
        yg = _s5_mixer(ug, s5_ops, s5_d[l], l, batch, n_lat, n_ctx)
        n_tiles = n_lat_tiles + 1 if l < depth - 1 else n_lat_tiles
        x_all = _mix_ffn(x_all, mods_l, ya, yb, yg, s5_w_glu, w_out, ffn2_w_in, ffn2_w_out, l, tiles_per_batch, n_tiles)
    return x_all.reshape(batch, n_lat, d)
```

```python
import functools
import math

import numpy as np
import jax
import jax.numpy as jnp
from jax import lax
from jax.experimental import pallas as pl
from jax.experimental.pallas import tpu as pltpu

F32 = jnp.float32
MXU_DTYPE = jnp.bfloat16

GRID_W = 64
N_MOD = 9
NORM_EPS = 1e-6
NA_HEADS, NA_HEAD_DIM, NA_ROWS, NA_COLS = 8, 32, 8, 16
RET_HEADS, RET_QK_DIM, RET_V_DIM = 4, 64, 128
ROPE_BASE = 10000.0
S5_GROUPS, S5_GROUP_CH, S5_STATE = 16, 16, 64
NA_WIDTH = NA_HEADS * NA_HEAD_DIM
RET_QK_WIDTH = RET_HEADS * RET_QK_DIM
RET_V_WIDTH = RET_HEADS * RET_V_DIM
S5_WIDTH = S5_GROUPS * S5_GROUP_CH

TOKEN_TILE = 512
FFN_CHUNK = 256
NA_BLOCK_ROWS = 4
NA_BLOCK = NA_BLOCK_ROWS * GRID_W
RET_CHUNK = 256
S5_CHUNK = 32
MASK_VALUE = -1e30
LOG2_E = math.log2(math.e)
LANES = 128
VMEM_LIMIT = 56 * 1024 * 1024


def _mxu(a, b):
    return jnp.dot(a.astype(MXU_DTYPE), b.astype(MXU_DTYPE), preferred_element_type=F32)


def _mxu_nt(a, b):
    return lax.dot_general(a.astype(MXU_DTYPE), b.astype(MXU_DTYPE), (((1,), (1,)), ((), ())),
                           preferred_element_type=F32)


def _mxu_tn(a, b):
    return lax.dot_general(a.astype(MXU_DTYPE), b.astype(MXU_DTYPE), (((0,), (0,)), ((), ())),
                           preferred_element_type=F32)


def _silu(x):
    return x * jax.nn.sigmoid(x)


def _cparams(*sem):
    return pltpu.CompilerParams(dimension_semantics=sem, vmem_limit_bytes=VMEM_LIMIT)


def _mod_kernel(s_ref, w_ref, b_ref, o_ref):
    s = _silu(s_ref[...])
    o_ref[...] = _mxu(s, w_ref[...]) + b_ref[...]


def _modulation(cond_rows, w_mod, b_mod):
    depth, d, nd = w_mod.shape
    tn = 1024
    return pl.pallas_call(
        _mod_kernel,
        grid=(depth, nd // tn),
        in_specs=[pl.BlockSpec((8, d), lambda l, j: (0, 0)),
                  pl.BlockSpec((None, d, tn), lambda l, j: (l, 0, j)),
                  pl.BlockSpec((None, 1, tn), lambda l, j: (l, 0, j))],
        out_specs=pl.BlockSpec((None, 8, tn), lambda l, j: (l, 0, j)),
        out_shape=jax.ShapeDtypeStruct((depth, 8, nd), F32),
        compiler_params=_cparams("arbitrary", "arbitrary"),
        name="modulation",
    )(cond_rows, w_mod, b_mod.reshape(depth, 1, nd))


def _mod_rows(mod_ref, row, j, d):
    return mod_ref[pl.ds(row, 1), j * d:(j + 1) * d]


def _norm_mod(x, shift, scale):
    ms = jnp.mean(x * x, axis=-1, keepdims=True)
    return (x * lax.rsqrt(ms + NORM_EPS)) * (1.0 + scale) + shift


def _ffn_body(x, mod_ref, row, j0, win_ref, wout_ref):
    d = x.shape[1]
    dff = wout_ref.shape[0]
    h = _norm_mod(x, _mod_rows(mod_ref, row, j0, d), _mod_rows(mod_ref, row, j0 + 1, d)).astype(MXU_DTYPE)
    acc = jnp.zeros(x.shape, F32)
    for c in range(dff // FFN_CHUNK):
        lo = c * FFN_CHUNK
        a = _mxu(h, win_ref[:, lo:lo + FFN_CHUNK])
        b = _mxu(h, win_ref[:, dff + lo:dff + lo + FFN_CHUNK])
        acc = acc + _mxu(_silu(a) * b, wout_ref[lo:lo + FFN_CHUNK, :])
    return x + (0.5 * _mod_rows(mod_ref, row, j0 + 2, d)) * acc


def _ffn_kernel(x_ref, mod_ref, win_ref, wout_ref, o_ref, *, tiles_per_batch):
    row = pl.program_id(0) // tiles_per_batch
    o_ref[...] = _ffn_body(x_ref[...], mod_ref, row, 0, win_ref, wout_ref)


def _ffn_split_kernel(x_ref, ctx_ref, mod_ref, win_ref, wout_ref, o_ref, *, tiles_per_batch, n_lat_tiles):
    i = pl.program_id(0)
    x = jnp.where(i < n_lat_tiles, x_ref[...], ctx_ref[...])
    o_ref[...] = _ffn_body(x, mod_ref, i // tiles_per_batch, 0, win_ref, wout_ref)


def _resident(shape, layer):
    return pl.BlockSpec((None,) + tuple(shape[1:]), lambda i: (layer,) + (0,) * (len(shape) - 1),
                        pipeline_mode=pl.Buffered(1))


def _ffn_first(x_lat, x_ctx, mods_l, w_in, w_out, layer, tiles_per_batch):
    d = x_lat.shape[1]
    tm = TOKEN_TILE
    tok = pl.BlockSpec((tm, d), lambda i: (i, 0))
    mod = pl.BlockSpec(mods_l.shape, lambda i: (0, 0))
    weights = [_resident(w_in.shape, layer), _resident(w_out.shape, layer)]
    if x_ctx is None:
        t_all = x_lat.shape[0]
        kern = functools.partial(_ffn_kernel, tiles_per_batch=tiles_per_batch)
        in_specs, args = [tok, mod] + weights, (x_lat, mods_l, w_in, w_out)
    else:
        n_lat_tiles = x_lat.shape[0] // tm
        t_all = x_lat.shape[0] + x_ctx.shape[0]
        kern = functools.partial(_ffn_split_kernel, tiles_per_batch=tiles_per_batch, n_lat_tiles=n_lat_tiles)
        in_specs = [pl.BlockSpec((tm, d), lambda i: (jnp.minimum(i, n_lat_tiles - 1), 0)),
                    pl.BlockSpec((tm, d), lambda i: (0, 0)), mod] + weights
        args = (x_lat, x_ctx, mods_l, w_in, w_out)
    return pl.pallas_call(
        kern, grid=(t_all // tm,), in_specs=in_specs, out_specs=tok,
        out_shape=jax.ShapeDtypeStruct((t_all, d), F32),
        compiler_params=_cparams("arbitrary"), name="ffn1",
    )(*args)


def _transpose_blocks(v):
    nb = len(v)
    lane_blk = lax.broadcasted_iota(jnp.int32, v[0].shape, 1) // S5_GROUP_CH
    s = nb // 2
    while s >= 1:
        upper = (lane_blk & s) != 0
        out = list(v)
        for a in range(nb):
            if a & s:
                continue
            b = a + s
            out[a] = jnp.where(upper, pltpu.roll(v[b], s * S5_GROUP_CH, 1), v[a])
            out[b] = jnp.where(upper, v[b], pltpu.roll(v[a], LANES - s * S5_GROUP_CH, 1))
        v = out
        s //= 2
    return v


def _token_scratch(tm):
    return pltpu.VMEM((S5_WIDTH // LANES, tm, LANES), F32)


def _tokens_to_groups(tok_ref, ug_ref):
    L, nb = S5_CHUNK, LANES // S5_GROUP_CH
    n = tok_ref.shape[1] // L
    for hb in range(tok_ref.shape[0]):
        for q in range(L // nb):
            by_group = _transpose_blocks([tok_ref[hb, pl.ds(q * nb + r, n, stride=L), :] for r in range(nb)])
            for j in range(nb):
                ug_ref[hb * nb + j, :, q * LANES:(q + 1) * LANES] = by_group[j]


def _groups_to_tokens(yg_ref, tok_ref):
    L, nb = S5_CHUNK, LANES // S5_GROUP_CH
    n = tok_ref.shape[1] // L
    for hb in range(tok_ref.shape[0]):
        for q in range(L // nb):
            by_token = _transpose_blocks([yg_ref[hb * nb + j, :, q * LANES:(q + 1) * LANES] for j in range(nb)])
            for r in range(nb):
                tok_ref[hb, pl.ds(q * nb + r, n, stride=L), :] = by_token[r]


def _head_rms(z, head_sum, gain):
    zz = z * z
    hi = zz.astype(MXU_DTYPE)
    lo = zz - hi.astype(F32)
    ms = (_mxu(hi, head_sum) + _mxu(lo, head_sum)) * (1.0 / NA_HEAD_DIM)
    return z * lax.rsqrt(ms + NORM_EPS) * gain


def _rotate(z, cos, sin_signed):
    tm = z.shape[0]
    lane = lax.broadcasted_iota(jnp.int32, (tm, 128), 1)
    first_half = (lane % RET_QK_DIM) < (RET_QK_DIM // 2)
    outs = []
    for c in range(z.shape[1] // 128):
        zc = z[:, c * 128:(c + 1) * 128]
        partner = jnp.where(first_half, pltpu.roll(zc, 128 - RET_QK_DIM // 2, 1), pltpu.roll(zc, RET_QK_DIM // 2, 1))
        outs.append(zc * cos[:, c * 128:(c + 1) * 128] + partner * sin_signed[:, c * 128:(c + 1) * 128])
    return jnp.concatenate(outs, axis=1)


def _inproj_kernel(x_ref, mod_ref, w_ref, hs_ref, qg_ref, kg_ref, cos_ref, sin_ref,
                   qa_ref, ka_ref, va_ref, qb_ref, kb_ref, vb_ref, gb_ref, ug_ref, tok_ref, *, tiles_per_batch):
    d = x_ref.shape[1]
    row = pl.program_id(0) // tiles_per_batch
    x = x_ref[...]
    h = _norm_mod(x, _mod_rows(mod_ref, row, 3, d), _mod_rows(mod_ref, row, 4, d)).astype(MXU_DTYPE)

    def proj(lo, width):
        return _mxu(h, w_ref[:, lo:lo + width])

    head_sum = hs_ref[...]
    o = 0
    qa = _head_rms(proj(o, NA_WIDTH), head_sum, qg_ref[...]) * (NA_HEAD_DIM ** -0.5 * LOG2_E)
    qa_ref[...] = qa.astype(qa_ref.dtype)
    o += NA_WIDTH
    ka_ref[...] = _head_rms(proj(o, NA_WIDTH), head_sum, kg_ref[...]).astype(ka_ref.dtype)
    o += NA_WIDTH
    va_ref[...] = proj(o, NA_WIDTH).astype(va_ref.dtype)
    o += NA_WIDTH
    cos = cos_ref[...]
    sin = sin_ref[...]
    qb_ref[...] = _rotate(proj(o, RET_QK_WIDTH), cos, sin).astype(qb_ref.dtype)
    o += RET_QK_WIDTH
    kb_ref[...] = (_rotate(proj(o, RET_QK_WIDTH), cos, sin) * (RET_QK_DIM ** -0.5)).astype(kb_ref.dtype)
    o += RET_QK_WIDTH
    vb_ref[...] = proj(o, RET_V_WIDTH).astype(vb_ref.dtype)
    o += RET_V_WIDTH
    gb_ref[...] = proj(o, RET_V_WIDTH)
    o += RET_V_WIDTH
    u = proj(o, S5_WIDTH)
    for hb in range(S5_WIDTH // LANES):
        tok_ref[hb] = u[:, hb * LANES:(hb + 1) * LANES]
    _tokens_to_groups(tok_ref, ug_ref)


def _inproj(x_all, mods_l, w_in, layer, head_sum, q_gain, k_gain, cos_tab, sin_tab, tiles_per_batch, n_lat_tiles):
    t_all, d = x_all.shape
    tm = TOKEN_TILE
    tok = lambda w: pl.BlockSpec((tm, w), lambda i: (i, 0))
    full = lambda a: pl.BlockSpec(a.shape, lambda i: (0,) * a.ndim)
    tab = pl.BlockSpec((tm, RET_QK_WIDTH),
                       lambda i: (jnp.where(i < n_lat_tiles, i % tiles_per_batch, tiles_per_batch), 0))
    chunk_w = S5_CHUNK * S5_GROUP_CH
    token_major = lambda w, dt: (tok(w), jax.ShapeDtypeStruct((t_all, w), dt))
    outs = [token_major(NA_WIDTH, MXU_DTYPE), token_major(NA_WIDTH, MXU_DTYPE), token_major(NA_WIDTH, MXU_DTYPE),
            token_major(RET_QK_WIDTH, MXU_DTYPE), token_major(RET_QK_WIDTH, MXU_DTYPE),
            token_major(RET_V_WIDTH, MXU_DTYPE), token_major(RET_V_WIDTH, F32),
            (pl.BlockSpec((S5_GROUPS, tm // S5_CHUNK, chunk_w), lambda i: (0, i, 0)),
             jax.ShapeDtypeStruct((S5_GROUPS, t_all // S5_CHUNK, chunk_w), F32))]
    return pl.pallas_call(
        functools.partial(_inproj_kernel, tiles_per_batch=tiles_per_batch),
        grid=(t_all // tm,),
        in_specs=[tok(d), full(mods_l), _resident(w_in.shape, layer),
                  full(head_sum), full(q_gain), full(k_gain), tab, tab],
        out_specs=[o[0] for o in outs],
        out_shape=[o[1] for o in outs],
        scratch_shapes=[_token_scratch(tm)],
        compiler_params=_cparams("arbitrary"),
        name="inproj",
    )(x_all, mods_l, w_in, head_sum, q_gain, k_gain, cos_tab, sin_tab)


def _softmax_heads(q, parts):
    lane = lax.broadcasted_iota(jnp.int32, (1, NA_WIDTH), 1)
    acc = jnp.zeros((q.shape[0], NA_WIDTH), F32)
    for h in range(NA_HEADS):
        in_head = (lane // NA_HEAD_DIM) == h
        qh = jnp.where(in_head, q, jnp.zeros_like(q))
        scores = []
        for k, _, bias_fn in parts:
            s = _mxu_nt(qh, k)
            scores.append(s if bias_fn is None else s + bias_fn(h))
        m = functools.reduce(jnp.maximum, [jnp.max(s, axis=-1, keepdims=True) for s in scores])
        ps = [jnp.exp2(s - m) for s in scores]
        l = functools.reduce(jnp.add, [jnp.sum(p, axis=-1, keepdims=True) for p in ps])
        o = functools.reduce(jnp.add, [_mxu(p, v) for p, (_, v, _) in zip(ps, parts)])
        acc = acc + o * jnp.where(in_head, 1.0 / l, 0.0)
    return acc


def _na_kernel(q_ref, k0_ref, k1_ref, k2_ref, v0_ref, v1_ref, v2_ref, kc_ref, vc_ref, bias_ref, o_ref, *, nblk):
    i = pl.program_id(1)
    q = q_ref[...]
    kc = kc_ref[...]
    vc = vc_ref[...]

    @pl.when(i < nblk)
    def _():
        k = jnp.concatenate([k0_ref[...], k1_ref[...], k2_ref[...]], axis=0)
        v = jnp.concatenate([v0_ref[...], v1_ref[...], v2_ref[...]], axis=0)
        out = _softmax_heads(q, [(k, v, lambda h: bias_ref[h]), (kc, vc, None)])
        o_ref[...] = out.astype(o_ref.dtype)

    @pl.when(i == nblk)
    def _():
        o_ref[...] = _softmax_heads(q, [(kc, vc, None)]).astype(o_ref.dtype)


def _na_bias_tables(rpb, rows):
    depth = rpb.shape[0]
    nc = 2 * NA_COLS - 1
    lead = GRID_W - NA_COLS
    padded = jnp.pad(rpb.astype(F32) * LOG2_E, ((0, 0),) * 3 + ((lead, 2 * GRID_W - lead - nc),))
    flat = jnp.tile(padded, (1, 1, 1, GRID_W))[..., :GRID_W * (2 * GRID_W - 1)]
    col = flat.reshape(rpb.shape[:3] + (GRID_W, 2 * GRID_W - 1))[..., GRID_W - 1:]
    n_key_rows = 3 * NA_BLOCK_ROWS
    per_a = [col[:, :, NA_ROWS - 1 - NA_BLOCK_ROWS - a:][:, :, :n_key_rows] for a in range(NA_BLOCK_ROWS)]
    base = jnp.transpose(jnp.stack(per_a, axis=2), (0, 1, 2, 4, 3, 5))
    base = base.reshape(depth, NA_HEADS, NA_BLOCK, n_key_rows * GRID_W)
    w = np.arange(GRID_W)
    kc = np.arange(GRID_W)
    valid = np.zeros((3, NA_BLOCK, n_key_rows * GRID_W), bool)
    cs = np.clip(w - NA_COLS // 2, 0, GRID_W - NA_COLS)
    col_ok = (kc[None, :] >= cs[:, None]) & (kc[None, :] < cs[:, None] + NA_COLS)
    for var, r0 in enumerate((0, 2 * NA_BLOCK_ROWS, rows - NA_BLOCK_ROWS)):
        for a in range(NA_BLOCK_ROWS):
            r = r0 + a
            rs = np.clip(r - NA_ROWS // 2, 0, rows - NA_ROWS)
            kr = r0 - NA_BLOCK_ROWS + np.arange(n_key_rows)
            row_ok = (kr >= 0) & (kr < rows) & (kr >= rs) & (kr < rs + NA_ROWS)
            ok = row_ok[None, :, None] & col_ok[:, None, :]
            valid[var, a * GRID_W:(a + 1) * GRID_W] = ok.reshape(GRID_W, -1)
    return jnp.where(jnp.asarray(valid)[None, :, None], base[:, None], MASK_VALUE)


def _neighborhood_attention(qa, ka, va, bias, layer, batch, n_lat):
    t_all = qa.shape[0]
    nblk = n_lat // NA_BLOCK
    ctx0 = batch * nblk
    blk = (NA_BLOCK, NA_WIDTH)

    def qmap(b, i):
        return (jnp.where(i < nblk, b * nblk + i, ctx0 + b), 0)

    kspec = lambda off: pl.BlockSpec(blk, lambda b, i: (b * nblk + jnp.clip(i + off, 0, nblk - 1), 0))
    cspec = pl.BlockSpec(blk, lambda b, i: (ctx0 + b, 0))
    bmap = lambda b, i: (layer, jnp.where(i == 0, 0, jnp.where(i >= nblk - 1, 2, 1)), 0, 0, 0)
    return pl.pallas_call(
        functools.partial(_na_kernel, nblk=nblk),
        grid=(batch, nblk + 1),
        in_specs=[pl.BlockSpec(blk, qmap),
                  kspec(-1), kspec(0), kspec(1), kspec(-1), kspec(0), kspec(1), cspec, cspec,
                  pl.BlockSpec((None, None) + bias.shape[2:], bmap)],
        out_specs=pl.BlockSpec(blk, qmap),
        out_shape=jax.ShapeDtypeStruct((t_all, NA_WIDTH), MXU_DTYPE),
        compiler_params=_cparams("arbitrary", "arbitrary"),
        name="natten",
    )(qa, ka, ka, ka, va, va, va, ka, va, bias)


def _ret_state_update(state_ref, k, v, w_end, g_chunk, block_mask):
    kw = k.astype(F32) * w_end
    state_ref[...] = g_chunk * state_ref[...] + _mxu_tn(kw, v) * block_mask


def _ret_fwd_kernel(q_ref, k_ref, v_ref, dmat_ref, win_ref, wend_ref, gch_ref, bm_ref, o_ref, state_ref):
    @pl.when(pl.program_id(1) == 0)
    def _():
        state_ref[...] = jnp.zeros_like(state_ref)

    q = q_ref[...]
    k = k_ref[...]
    v = v_ref[...]
    cross = _mxu(q, state_ref[...]) * win_ref[...]
    lane = lax.broadcasted_iota(jnp.int32, (1, RET_QK_WIDTH), 1)
    for h in range(RET_HEADS):
        qh = jnp.where((lane // RET_QK_DIM) == h, q, jnp.zeros_like(q))
        a = _mxu_nt(qh, k) * dmat_ref[h]
        sl = slice(h * RET_V_DIM, (h + 1) * RET_V_DIM)
        o_ref[:, sl] = _mxu(a, v[:, sl]) + cross[:, sl]
    _ret_state_update(state_ref, k, v, wend_ref[...], gch_ref[...], bm_ref[...])


def _ret_bwd_kernel(q_ref, k_ref, v_ref, of_ref, gate_ref, win_ref, wend_ref, gch_ref, bm_ref, gn_ref,
                    y_ref, state_ref):
    @pl.when(pl.program_id(1) == 0)
    def _():
        state_ref[...] = jnp.zeros_like(state_ref)

    q = q_ref[...]
    o = of_ref[...] + _mxu(q, state_ref[...]) * win_ref[...]
    gate = _silu(gate_ref[...])
    gain = gn_ref[...]
    for h in range(RET_HEADS):
        sl = slice(h * RET_V_DIM, (h + 1) * RET_V_DIM)
        oh = o[:, sl]
        mu = jnp.mean(oh, axis=-1, keepdims=True)
        cen = oh - mu
        var = jnp.mean(cen * cen, axis=-1, keepdims=True)
        y_ref[:, sl] = (gate[:, sl] * (cen * lax.rsqrt(var + NORM_EPS) * gain[:, sl])).astype(y_ref.dtype)
    _ret_state_update(state_ref, k_ref[...], v_ref[...], wend_ref[...], gch_ref[...], bm_ref[...])


def _ret_tables(decay):
    c = RET_CHUNK
    log_g = jax.nn.log_sigmoid(decay.astype(F32))
    pos = jnp.arange(c, dtype=F32)
    diff = pos[:, None] - pos[None, :]
    lgf = log_g[0][:, None, None]
    lgb = log_g[1][:, None, None]
    dmat = jnp.where(diff >= 0, jnp.exp(lgf * jnp.maximum(diff, 0.0)), jnp.exp(lgb * jnp.maximum(-diff, 0.0)))
    rep = lambda a, w: jnp.repeat(a, w, axis=-1)
    win_f = rep(jnp.exp(log_g[0][None, :] * (pos[:, None] + 1.0)), RET_V_DIM)
    wend_f = rep(jnp.exp(log_g[0][None, :] * (c - 1.0 - pos[:, None])), RET_QK_DIM)
    win_b = rep(jnp.exp(log_g[1][None, :] * (c - pos[:, None])), RET_V_DIM)
    wend_b = rep(jnp.exp(log_g[1][None, :] * pos[:, None]), RET_QK_DIM)
    gch_f = rep(jnp.exp(log_g[0] * c)[None, :], RET_V_DIM)
    gch_b = rep(jnp.exp(log_g[1] * c)[None, :], RET_V_DIM)
    return dmat, (win_f, wend_f, gch_f), (win_b, wend_b, gch_b)


def _retention(qb, kb, vb, gb, tables, gn, layer, batch, n_lat):
    t_all = qb.shape[0]
    c = RET_CHUNK
    nch = n_lat // c
    ctx0 = batch * nch
    dmat, (win_f, wend_f, gch_f), (win_b, wend_b, gch_b) = tables
    bm = (np.arange(RET_QK_WIDTH)[:, None] // RET_QK_DIM == np.arange(RET_V_WIDTH)[None, :] // RET_V_DIM)
    bm = jnp.asarray(bm, F32)
    fmap = lambda b, i: (jnp.where(i == 0, ctx0 + b, b * nch + i - 1), 0)
    rmap = lambda b, i: (jnp.where(i == 0, ctx0 + b, b * nch + nch - i), 0)

    def full(a):
        if a.ndim == 2:
            return pl.BlockSpec(a.shape, lambda b, i: (0, 0))
        return pl.BlockSpec((None,) + a.shape[1:], lambda b, i: (layer,) + (0,) * (a.ndim - 1))

    spec = lambda w, m: pl.BlockSpec((c, w), m)
    state = pltpu.VMEM((RET_QK_WIDTH, RET_V_WIDTH), F32)
    o_f = pl.pallas_call(
        _ret_fwd_kernel,
        grid=(batch, nch + 1),
        in_specs=[spec(RET_QK_WIDTH, fmap), spec(RET_QK_WIDTH, fmap), spec(RET_V_WIDTH, fmap),
                  full(dmat), full(win_f), full(wend_f), full(gch_f), full(bm)],
        out_specs=spec(RET_V_WIDTH, fmap),
        out_shape=jax.ShapeDtypeStruct((t_all, RET_V_WIDTH), F32),
        scratch_shapes=[state],
        compiler_params=_cparams("arbitrary", "arbitrary"),
        name="retention_fwd",
    )(qb, kb, vb, dmat, win_f, wend_f, gch_f, bm)
    return pl.pallas_call(
        _ret_bwd_kernel,
        grid=(batch, nch + 1),
        in_specs=[spec(RET_QK_WIDTH, rmap), spec(RET_QK_WIDTH, rmap), spec(RET_V_WIDTH, rmap),
                  spec(RET_V_WIDTH, rmap), spec(RET_V_WIDTH, rmap),
                  full(win_b), full(wend_b), full(gch_b), full(bm), full(gn)],
        out_specs=spec(RET_V_WIDTH, rmap),
        out_shape=jax.ShapeDtypeStruct((t_all, RET_V_WIDTH), MXU_DTYPE),
        scratch_shapes=[state],
        compiler_params=_cparams("arbitrary", "arbitrary"),
        name="retention_bwd",
    )(qb, kb, vb, o_f, gb, win_b, wend_b, gch_b, bm, gn)


def _s5_discretize(a_re, a_im, log_dt, b_re, b_im):
    a_re = jnp.minimum(a_re.astype(F32), -1e-4)
    a_im = a_im.astype(F32)
    dt = jnp.exp(log_dt.astype(F32))[..., None]
    mag = jnp.exp(dt * a_re)
    ab_re = mag * jnp.cos(dt * a_im)
    ab_im = mag * jnp.sin(dt * a_im)
    den = a_re * a_re + a_im * a_im
    nr = ab_re - 1.0
    f_re = ((nr * a_re + ab_im * a_im) / den)[..., None]
    f_im = ((ab_im * a_re - nr * a_im) / den)[..., None]
    br = b_re.astype(F32)[None]
    bi = b_im.astype(F32)[None]
    return ab_re, ab_im, f_re * br - f_im * bi, f_re * bi + f_im * br


def _complex_powers(ar, ai, n):
    pr = jnp.ones((1,) + ar.shape, F32)
    pi = jnp.zeros((1,) + ar.shape, F32)
    cr, ci = ar, ai
    while pr.shape[0] < n:
        pr, pi = (jnp.concatenate([pr, pr * cr - pi * ci], axis=0),
                  jnp.concatenate([pi, pr * ci + pi * cr], axis=0))
        cr, ci = cr * cr - ci * ci, 2.0 * cr * ci
    return pr[:n], pi[:n]


def _s5_operators(a_re, a_im, log_dt, b_re, b_im, c_re, c_im, levels):
    G, P, Cg, L = S5_GROUPS, S5_STATE, S5_GROUP_CH, S5_CHUNK
    ab_re, ab_im, bb_re, bb_im = _s5_discretize(a_re, a_im, log_dt, b_re, b_im)
    pr, pi = _complex_powers(ab_re, ab_im, L + 1)
    cr = c_re.astype(F32)
    ci = c_im.astype(F32)
    bt = jnp.concatenate([jnp.swapaxes(bb_re, 2, 3), jnp.swapaxes(bb_im, 2, 3)], axis=3)

    def state_in(n, exps):
        wr = jnp.einsum('sgp,gpc->gscp', pr[exps, n], bb_re[n]) - jnp.einsum('sgp,gpc->gscp', pi[exps, n], bb_im[n])
        wi = jnp.einsum('sgp,gpc->gscp', pr[exps, n], bb_im[n]) + jnp.einsum('sgp,gpc->gscp', pi[exps, n], bb_re[n])
        return (jnp.concatenate([wr, wi], axis=-1).reshape(G, L * Cg, 2 * P),
                jnp.concatenate([wi, wr], axis=-1).reshape(G, L * Cg, 2 * P))

    def state_out(n, exps):
        er = jnp.einsum('tgp,gcp->gptc', pr[exps, n], cr[n]) - jnp.einsum('tgp,gcp->gptc', pi[exps, n], ci[n])
        ei = jnp.einsum('tgp,gcp->gptc', pr[exps, n], ci[n]) + jnp.einsum('tgp,gcp->gptc', pi[exps, n], cr[n])
        return jnp.concatenate([er, -ei], axis=1).reshape(G, 2 * P, L * Cg)

    s = np.arange(L)
    ca = jnp.stack([state_out(0, s), state_out(1, L - 1 - s)])
    ff, ffs = state_in(0, L - 1 - s)
    fb, fbs = state_in(1, s)
    f = jnp.concatenate([ff, ffs, fb, fbs], axis=-1).astype(MXU_DTYPE)
    e = jnp.stack([state_out(0, s + 1), state_out(1, L - s)]).astype(MXU_DTYPE)
    return bt, ca, f, e, _s5_scan_mults(pr[L], pi[L], levels)


def _s5_scan_mults(alr, ali, levels):
    rows = []
    cr, ci = alr, ali
    for _ in range(levels):
        rows.append(jnp.stack([jnp.concatenate([cr, cr], -1), jnp.concatenate([-ci, ci], -1),
                               jnp.concatenate([ci, -ci], -1)], axis=-2))
        cr, ci = cr * cr - ci * ci, 2.0 * cr * ci
    return jnp.stack(rows, axis=2)


def _shift_rows(x, k, down):
    n = x.shape[0]
    row = lax.broadcasted_iota(jnp.int32, (n, 1), 0)
    if down:
        return jnp.where(row >= k, pltpu.roll(x, k, 0), 0.0)
    return jnp.where(row < n - k, pltpu.roll(x, n - k, 0), 0.0)


def _s5_scan(loc, locs, mult_ref, init, inits, forward):
    n = loc.shape[0]
    a1, a2, a2s = mult_ref[0, 0:1], mult_ref[0, 1:2], mult_ref[0, 2:3]
    first = (lax.broadcasted_iota(jnp.int32, (n, 1), 0) == (0 if forward else n - 1)).astype(F32)
    x = loc + first * (a1 * init + a2 * inits)
    xs = locs + first * (a1 * inits + a2s * init)
    k, lvl = 1, 0
    while k < n:
        a1, a2, a2s = mult_ref[lvl, 0:1], mult_ref[lvl, 1:2], mult_ref[lvl, 2:3]
        sx, sxs = _shift_rows(x, k, forward), _shift_rows(xs, k, forward)
        x, xs = x + a1 * sx + a2 * sxs, xs + a1 * sxs + a2s * sx
        k, lvl = 2 * k, lvl + 1
    last = slice(n - 1, n) if forward else slice(0, 1)
    prev = _shift_rows(x, 1, forward) + first * init
    return prev, x[last], xs[last]


def _s5_toeplitz(bt_ref, ca_ref, conv_ref):
    L, Cg = S5_CHUNK, S5_GROUP_CH
    width = L * Cg
    hi = lax.Precision.HIGHEST
    kf = jnp.dot(bt_ref[0], ca_ref[0], preferred_element_type=F32, precision=hi)
    kb = jnp.dot(bt_ref[1], ca_ref[1], preferred_element_type=F32, precision=hi)
    lane = lax.broadcasted_iota(jnp.int32, (Cg, width), 1)
    for s in range(L):
        right = s * Cg
        left = (L - 1 - s) * Cg
        fwd = kf if right == 0 else jnp.where(lane >= right, pltpu.roll(kf, right, 1), 0.0)
        bwd = kb if left == 0 else jnp.where(lane < width - left, pltpu.roll(kb, width - left, 1), 0.0)
        conv_ref[s * Cg:(s + 1) * Cg, :] = fwd + bwd


def _s5_kernel(u_ref, bt_ref, ca_ref, fcat_ref, e_ref, mult_ref, d_ref, y_ref, conv_ref, *, batch, n_lat_ch, n_ctx_ch):
    P2 = 2 * S5_STATE
    _s5_toeplitz(bt_ref, ca_ref, conv_ref)
    u = u_ref[...]
    ub = u.astype(MXU_DTYPE)
    y_ref[...] = _mxu(ub, conv_ref[...]) + u * d_ref[...]
    loc = _mxu(ub, fcat_ref[...])
    zero = jnp.zeros((1, P2), F32)
    for b in range(batch):
        ctx = slice(batch * n_lat_ch + b * n_ctx_ch, batch * n_lat_ch + (b + 1) * n_ctx_ch)
        lat = slice(b * n_lat_ch, (b + 1) * n_lat_ch)
        for n, fwd in ((0, True), (1, False)):
            c0 = n * 2 * P2
            m_ref = mult_ref.at[n]
            pc, xl, xls = _s5_scan(loc[ctx, c0:c0 + P2], loc[ctx, c0 + P2:c0 + 2 * P2], m_ref, zero, zero, fwd)
            pl_, _, _ = _s5_scan(loc[lat, c0:c0 + P2], loc[lat, c0 + P2:c0 + 2 * P2], m_ref, xl, xls, fwd)
            y_ref[ctx, :] += _mxu(pc, e_ref[n])
            y_ref[lat, :] += _mxu(pl_, e_ref[n])


def _s5_mixer(ug, ops, d_skip, layer, batch, n_lat, n_ctx):
    bt, ca, f_cat, e, mults = ops
    G, Cg, L, P2 = S5_GROUPS, S5_GROUP_CH, S5_CHUNK, 2 * S5_STATE
    rows, width = ug.shape[1], L * Cg
    levels = mults.shape[3]
    per_g = lambda *s: pl.BlockSpec((None,) + s, lambda g: (g,) + (0,) * len(s))
    per_dir = lambda *s: pl.BlockSpec((None, 2, None) + s, lambda g: (layer, 0, g) + (0,) * len(s))
    return pl.pallas_call(
        functools.partial(_s5_kernel, batch=batch, n_lat_ch=n_lat // L, n_ctx_ch=n_ctx // L),
        grid=(G,),
        in_specs=[per_g(rows, width), per_dir(Cg, P2), per_dir(P2, width),
                  pl.BlockSpec((None, None, width, 4 * P2), lambda g: (layer, g, 0, 0)),
                  per_dir(P2, width), per_dir(levels, 3, P2),
                  pl.BlockSpec((None, None, 1, width), lambda g: (layer, g, 0, 0))],
        out_specs=per_g(rows, width),
        out_shape=jax.ShapeDtypeStruct((G, rows, width), F32),
        scratch_shapes=[pltpu.VMEM((width, width), F32)],
        compiler_params=_cparams("arbitrary"),
        name="s5_conv",
    )(ug, bt, ca, f_cat, e, mults, d_skip)


def _mix_ffn_kernel(x_ref, mod_ref, ya_ref, yb_ref, yg_ref, wglu_ref, wo_ref, win_ref, wout_ref, o_ref, tok_ref,
                    *, tiles_per_batch):
    d = x_ref.shape[1]
    row = pl.program_id(0) // tiles_per_batch
    _groups_to_tokens(yg_ref, tok_ref)
    yc = jnp.concatenate([tok_ref[hb] for hb in range(S5_WIDTH // LANES)], axis=1)
    glu = _mxu(jax.nn.gelu(yc), wglu_ref[...])
    s5 = glu[:, :S5_WIDTH] * jax.nn.sigmoid(glu[:, S5_WIDTH:])
    mix = (_mxu(ya_ref[...], wo_ref[0:NA_WIDTH, :])
           + _mxu(yb_ref[...], wo_ref[NA_WIDTH:NA_WIDTH + RET_V_WIDTH, :])
           + _mxu(s5, wo_ref[NA_WIDTH + RET_V_WIDTH:, :]))
    x = x_ref[...] + _mod_rows(mod_ref, row, 5, d) * mix
    o_ref[...] = _ffn_body(x, mod_ref, row, 6, win_ref, wout_ref)


def _mix_ffn(x_all, mods_l, ya, yb, yg, w_glu, w_out, w_in2, w_out2, layer, tiles_per_batch, n_tiles):
    d = x_all.shape[1]
    tm = TOKEN_TILE
    tok = lambda w: pl.BlockSpec((tm, w), lambda i: (i, 0))
    return pl.pallas_call(
        functools.partial(_mix_ffn_kernel, tiles_per_batch=tiles_per_batch),
        grid=(n_tiles,),
        in_specs=[tok(d), pl.BlockSpec(mods_l.shape, lambda i: (0, 0)), tok(NA_WIDTH), tok(RET_V_WIDTH),
                  pl.BlockSpec((S5_GROUPS, tm // S5_CHUNK, yg.shape[2]), lambda i: (0, i, 0)),
                  _resident(w_glu.shape, layer), _resident(w_out.shape, layer),
                  _resident(w_in2.shape, layer), _resident(w_out2.shape, layer)],
        out_specs=tok(d),
        out_shape=jax.ShapeDtypeStruct((n_tiles * tm, d), F32),
        scratch_shapes=[_token_scratch(tm)],
        compiler_params=_cparams("arbitrary"),
        name="mix_ffn2",
    )(x_all, mods_l, ya, yb, yg, w_glu, w_out, w_in2, w_out2)


def _rope_tables(n_lat, n_ctx):
    nf = RET_QK_DIM // 4
    inv = ROPE_BASE ** (-jnp.arange(nf, dtype=F32) / nf)
    t = jnp.arange(n_lat)
    row = (t // GRID_W).astype(F32)
    col = (t % GRID_W).astype(F32)
    ang = jnp.concatenate([row[:, None] * inv, col[:, None] * inv], axis=-1)
    cos, sin = jnp.cos(ang), jnp.sin(ang)
    cos_l = jnp.tile(jnp.concatenate([cos, cos], axis=-1), (1, RET_HEADS))
    sin_l = jnp.tile(jnp.concatenate([-sin, sin], axis=-1), (1, RET_HEADS))
    pad = TOKEN_TILE
    return (jnp.concatenate([cos_l, jnp.ones((pad, RET_QK_WIDTH), F32)], axis=0),
            jnp.concatenate([sin_l, jnp.zeros((pad, RET_QK_WIDTH), F32)], axis=0))


def kernel(x, c, ctx, c_ctx, w_mod, b_mod, ffn1_w_in, ffn1_w_out, w_in, w_out, na_q_gain, na_k_gain, na_rpb,
           ret_decay, ret_gn, s5_a_re, s5_a_im, s5_log_dt, s5_b_re, s5_b_im, s5_c_re, s5_c_im, s5_d, s5_w_glu,
           ffn2_w_in, ffn2_w_out):
    batch, n_lat, d = x.shape
    n_ctx = ctx.shape[1]
    depth = w_mod.shape[0]
    assert n_ctx == NA_BLOCK == RET_CHUNK and batch * n_ctx == TOKEN_TILE
    assert n_lat % TOKEN_TILE == 0 and n_lat // NA_BLOCK >= 4
    assert batch + 1 <= 8
    tiles_per_batch = n_lat // TOKEN_TILE
    n_lat_tiles = batch * tiles_per_batch
    rows = n_lat // GRID_W

    bf = lambda a: a.astype(MXU_DTYPE)
    ffn1_w_in, ffn1_w_out, ffn2_w_in, ffn2_w_out = bf(ffn1_w_in), bf(ffn1_w_out), bf(ffn2_w_in), bf(ffn2_w_out)
    w_in, w_out, s5_w_glu = bf(w_in), bf(w_out), bf(s5_w_glu)

    cond = jnp.concatenate([c, c_ctx[None, :], jnp.zeros((8 - batch - 1, d), F32)], axis=0)
    mods = _modulation(cond, w_mod, b_mod)

    head_sum = jnp.asarray(np.arange(NA_WIDTH)[:, None] // NA_HEAD_DIM == np.arange(NA_WIDTH)[None, :] // NA_HEAD_DIM,
                           MXU_DTYPE)
    cos_tab, sin_tab = _rope_tables(n_lat, n_ctx)
    levels = max(1, int(math.ceil(math.log2(max(n_lat, n_ctx) // S5_CHUNK))))
    s5_ops = jax.vmap(functools.partial(_s5_operators, levels=levels))(
        s5_a_re, s5_a_im, s5_log_dt, s5_b_re, s5_b_im, s5_c_re, s5_c_im)

    na_bias = _na_bias_tables(na_rpb, rows)
    ret_tables = jax.vmap(_ret_tables)(ret_decay)
    ret_gain = ret_gn.astype(F32).reshape(depth, 1, RET_V_WIDTH)
    q_gain = jnp.tile(na_q_gain.astype(F32), (1, NA_HEADS)).reshape(depth, 1, NA_WIDTH)
    k_gain = jnp.tile(na_k_gain.astype(F32), (1, NA_HEADS)).reshape(depth, 1, NA_WIDTH)
    s5_skip = jnp.tile(s5_d.astype(F32).reshape(depth, S5_GROUPS, 1, S5_GROUP_CH), (1, 1, 1, S5_CHUNK))

    x_all = x.reshape(batch * n_lat, d)
    x_ctx = ctx.reshape(batch * n_ctx, d)
    for l in range(depth):
        mods_l = mods[l]
        x_all = _ffn_first(x_all, x_ctx if l == 0 else None, mods_l, ffn1_w_in, ffn1_w_out, l, tiles_per_batch)
        qa, ka, va, qb, kb, vb, gb, ug = _inproj(x_all, mods_l, w_in, l, head_sum, q_gain[l], k_gain[l],
                                                 cos_tab, sin_tab, tiles_per_batch, n_lat_tiles)
        ya = _neighborhood_attention(qa, ka, va, na_bias, l, batch, n_lat)
        yb = _retention(qb, kb, vb, gb, ret_tables, ret_gain, l, batch, n_lat)
        yg = _s5_mixer(ug, s5_ops, s5_skip, l, batch, n_lat, n_ctx)
        n_tiles = n_lat_tiles + 1 if l < depth - 1 else n_lat_tiles
        x_all = _mix_ffn(x_all, mods_l, ya, yb, yg, s5_w_glu, w_out, ffn2_w_in, ffn2_w_out, l, tiles_per_batch, n_tiles)
    return x_all.reshape(batch, n_lat, d)
```

```python
import functools
import math

import numpy as np
import jax
import jax.numpy as jnp
from jax import lax
from jax.experimental import pallas as pl
from jax.experimental.pallas import tpu as pltpu

F32 = jnp.float32
MXU_DTYPE = jnp.bfloat16

GRID_W = 64
N_MOD = 9
NORM_EPS = 1e-6
NA_HEADS, NA_HEAD_DIM, NA_ROWS, NA_COLS = 8, 32, 8, 16
RET_HEADS, RET_QK_DIM, RET_V_DIM = 4, 64, 128
ROPE_BASE = 10000.0
S5_GROUPS, S5_GROUP_CH, S5_STATE = 16, 16, 64
NA_WIDTH = NA_HEADS * NA_HEAD_DIM
RET_QK_WIDTH = RET_HEADS * RET_QK_DIM
RET_V_WIDTH = RET_HEADS * RET_V_DIM
S5_WIDTH = S5_GROUPS * S5_GROUP_CH

TOKEN_TILE = 512
FFN_CHUNK = 256
NA_BLOCK_ROWS = 4
NA_BLOCK = NA_BLOCK_ROWS * GRID_W
RET_CHUNK = 256
S5_CHUNK = 32
MASK_VALUE = -1e30
LOG2_E = math.log2(math.e)
LANES = 128
VMEM_LIMIT = 56 * 1024 * 1024


def _mxu(a, b):
    return jnp.dot(a.astype(MXU_DTYPE), b.astype(MXU_DTYPE), preferred_element_type=F32)


def _mxu_nt(a, b):
    return lax.dot_general(a.astype(MXU_DTYPE), b.astype(MXU_DTYPE), (((1,), (1,)), ((), ())),
                           preferred_element_type=F32)


def _mxu_tn(a, b):
    return lax.dot_general(a.astype(MXU_DTYPE), b.astype(MXU_DTYPE), (((0,), (0,)), ((), ())),
                           preferred_element_type=F32)


def _silu(x):
    return x * jax.nn.sigmoid(x)


def _cparams(*sem):
    return pltpu.CompilerParams(dimension_semantics=sem, vmem_limit_bytes=VMEM_LIMIT)


def _mod_kernel(s_ref, w_ref, b_ref, o_ref):
    s = _silu(s_ref[...])
    o_ref[...] = _mxu(s, w_ref[...]) + b_ref[...]


def _modulation(cond_rows, w_mod, b_mod):
    depth, d, nd = w_mod.shape
    tn = 1024
    return pl.pallas_call(
        _mod_kernel,
        grid=(depth, nd // tn),
        in_specs=[pl.BlockSpec((8, d), lambda l, j: (0, 0)),
                  pl.BlockSpec((None, d, tn), lambda l, j: (l, 0, j)),
                  pl.BlockSpec((None, 1, tn), lambda l, j: (l, 0, j))],
        out_specs=pl.BlockSpec((None, 8, tn), lambda l, j: (l, 0, j)),
        out_shape=jax.ShapeDtypeStruct((depth, 8, nd), F32),
        compiler_params=_cparams("arbitrary", "arbitrary"),
        name="modulation",
    )(cond_rows, w_mod, b_mod.reshape(depth, 1, nd))


def _mod_rows(mod_ref, row, j, d):
    return mod_ref[pl.ds(row, 1), j * d:(j + 1) * d]


def _norm_mod(x, shift, scale):
    ms = jnp.mean(x * x, axis=-1, keepdims=True)
    return (x * lax.rsqrt(ms + NORM_EPS)) * (1.0 + scale) + shift


def _ffn_body(x, mod_ref, row, j0, win_ref, wout_ref):
    d = x.shape[1]
    dff = wout_ref.shape[0]
    h = _norm_mod(x, _mod_rows(mod_ref, row, j0, d), _mod_rows(mod_ref, row, j0 + 1, d)).astype(MXU_DTYPE)
    acc = jnp.zeros(x.shape, F32)
    for c in range(dff // FFN_CHUNK):
        lo = c * FFN_CHUNK
        a = _mxu(h, win_ref[:, lo:lo + FFN_CHUNK])
        b = _mxu(h, win_ref[:, dff + lo:dff + lo + FFN_CHUNK])
        acc = acc + _mxu(_silu(a) * b, wout_ref[lo:lo + FFN_CHUNK, :])
    return x + (0.5 * _mod_rows(mod_ref, row, j0 + 2, d)) * acc


def _ffn_kernel(x_ref, mod_ref, win_ref, wout_ref, o_ref, *, tiles_per_batch):
    row = pl.program_id(0) // tiles_per_batch
    o_ref[...] = _ffn_body(x_ref[...], mod_ref, row, 0, win_ref, wout_ref)


def _ffn_split_kernel(x_ref, ctx_ref, mod_ref, win_ref, wout_ref, o_ref, *, tiles_per_batch, n_lat_tiles):
    i = pl.program_id(0)
    x = jnp.where(i < n_lat_tiles, x_ref[...], ctx_ref[...])
    o_ref[...] = _ffn_body(x, mod_ref, i // tiles_per_batch, 0, win_ref, wout_ref)


def _resident(shape, layer):
    return pl.BlockSpec((None,) + tuple(shape[1:]), lambda i: (layer,) + (0,) * (len(shape) - 1),
                        pipeline_mode=pl.Buffered(1))


def _ffn_first(x_lat, x_ctx, mods_l, w_in, w_out, layer, tiles_per_batch):
    d = x_lat.shape[1]
    tm = TOKEN_TILE
    tok = pl.BlockSpec((tm, d), lambda i: (i, 0))
    mod = pl.BlockSpec(mods_l.shape, lambda i: (0, 0))
    weights = [_resident(w_in.shape, layer), _resident(w_out.shape, layer)]
    if x_ctx is None:
        t_all = x_lat.shape[0]
        kern = functools.partial(_ffn_kernel, tiles_per_batch=tiles_per_batch)
        in_specs, args = [tok, mod] + weights, (x_lat, mods_l, w_in, w_out)
    else:
        n_lat_tiles = x_lat.shape[0] // tm
        t_all = x_lat.shape[0] + x_ctx.shape[0]
        kern = functools.partial(_ffn_split_kernel, tiles_per_batch=tiles_per_batch, n_lat_tiles=n_lat_tiles)
        in_specs = [pl.BlockSpec((tm, d), lambda i: (jnp.minimum(i, n_lat_tiles - 1), 0)),
                    pl.BlockSpec((tm, d), lambda i: (0, 0)), mod] + weights
        args = (x_lat, x_ctx, mods_l, w_in, w_out)
    return pl.pallas_call(
        kern, grid=(t_all // tm,), in_specs=in_specs, out_specs=tok,
        out_shape=jax.ShapeDtypeStruct((t_all, d), F32),
        compiler_params=_cparams("arbitrary"), name="ffn1",
    )(*args)


def _transpose_blocks(v):
    nb = len(v)
    lane_blk = lax.broadcasted_iota(jnp.int32, v[0].shape, 1) // S5_GROUP_CH
    s = nb // 2
    while s >= 1:
        upper = (lane_blk & s) != 0
        out = list(v)
        for a in range(nb):
            if a & s:
                continue
            b = a + s
            out[a] = jnp.where(upper, pltpu.roll(v[b], s * S5_GROUP_CH, 1), v[a])
            out[b] = jnp.where(upper, v[b], pltpu.roll(v[a], LANES - s * S5_GROUP_CH, 1))
        v = out
        s //= 2
    return v


def _token_scratch(tm):
    return pltpu.VMEM((S5_WIDTH // LANES, tm, LANES), F32)


def _tokens_to_groups(tok_ref, ug_ref):
    L, nb = S5_CHUNK, LANES // S5_GROUP_CH
    n = tok_ref.shape[1] // L
    for hb in range(tok_ref.shape[0]):
        for q in range(L // nb):
            by_group = _transpose_blocks([tok_ref[hb, pl.ds(q * nb + r, n, stride=L), :] for r in range(nb)])
            for j in range(nb):
                ug_ref[hb * nb + j, :, q * LANES:(q + 1) * LANES] = by_group[j]


def _groups_to_tokens(yg_ref, tok_ref):
    L, nb = S5_CHUNK, LANES // S5_GROUP_CH
    n = tok_ref.shape[1] // L
    for hb in range(tok_ref.shape[0]):
        for q in range(L // nb):
            by_token = _transpose_blocks([yg_ref[hb * nb + j, :, q * LANES:(q + 1) * LANES] for j in range(nb)])
            for r in range(nb):
                tok_ref[hb, pl.ds(q * nb + r, n, stride=L), :] = by_token[r]


def _head_rms(z, head_sum, gain):
    zz = z * z
    hi = zz.astype(MXU_DTYPE)
    lo = zz - hi.astype(F32)
    ms = (_mxu(hi, head_sum) + _mxu(lo, head_sum)) * (1.0 / NA_HEAD_DIM)
    return z * lax.rsqrt(ms + NORM_EPS) * gain


def _rotate(z, cos, sin_signed):
    tm = z.shape[0]
    lane = lax.broadcasted_iota(jnp.int32, (tm, 128), 1)
    first_half = (lane % RET_QK_DIM) < (RET_QK_DIM // 2)
    outs = []
    for c in range(z.shape[1] // 128):
        zc = z[:, c * 128:(c + 1) * 128]
        partner = jnp.where(first_half, pltpu.roll(zc, 128 - RET_QK_DIM // 2, 1), pltpu.roll(zc, RET_QK_DIM // 2, 1))
        outs.append(zc * cos[:, c * 128:(c + 1) * 128] + partner * sin_signed[:, c * 128:(c + 1) * 128])
    return jnp.concatenate(outs, axis=1)


def _inproj_kernel(x_ref, mod_ref, w_ref, hs_ref, qg_ref, kg_ref, cos_ref, sin_ref,
                   qa_ref, ka_ref, va_ref, qb_ref, kb_ref, vb_ref, gb_ref, ug_ref, tok_ref, *, tiles_per_batch):
    d = x_ref.shape[1]
    row = pl.program_id(0) // tiles_per_batch
    x = x_ref[...]
    h = _norm_mod(x, _mod_rows(mod_ref, row, 3, d), _mod_rows(mod_ref, row, 4, d)).astype(MXU_DTYPE)

    def proj(lo, width):
        return _mxu(h, w_ref[:, lo:lo + width])

    widths = (NA_WIDTH, NA_WIDTH, NA_WIDTH, RET_QK_WIDTH, RET_QK_WIDTH, RET_V_WIDTH, RET_V_WIDTH, S5_WIDTH)
    off = [sum(widths[:i]) for i in range(len(widths))]
    u = proj(off[7], S5_WIDTH)
    for hb in range(S5_WIDTH // LANES):
        tok_ref[hb] = u[:, hb * LANES:(hb + 1) * LANES]
    _tokens_to_groups(tok_ref, ug_ref)
    head_sum = hs_ref[...]
    qa = _head_rms(proj(off[0], NA_WIDTH), head_sum, qg_ref[...]) * (NA_HEAD_DIM ** -0.5 * LOG2_E)
    qa_ref[...] = qa.astype(qa_ref.dtype)
    ka_ref[...] = _head_rms(proj(off[1], NA_WIDTH), head_sum, kg_ref[...]).astype(ka_ref.dtype)
    cos = cos_ref[...]
    sin = sin_ref[...]
    qb_ref[...] = _rotate(proj(off[3], RET_QK_WIDTH), cos, sin).astype(qb_ref.dtype)
    kb_ref[...] = (_rotate(proj(off[4], RET_QK_WIDTH), cos, sin) * (RET_QK_DIM ** -0.5)).astype(kb_ref.dtype)
    va_ref[...] = proj(off[2], NA_WIDTH).astype(va_ref.dtype)
    vb_ref[...] = proj(off[5], RET_V_WIDTH).astype(vb_ref.dtype)
    gb_ref[...] = proj(off[6], RET_V_WIDTH)


def _inproj(x_all, mods_l, w_in, layer, head_sum, q_gain, k_gain, cos_tab, sin_tab, tiles_per_batch, n_lat_tiles):
    t_all, d = x_all.shape
    tm = TOKEN_TILE
    tok = lambda w: pl.BlockSpec((tm, w), lambda i: (i, 0))
    full = lambda a: pl.BlockSpec(a.shape, lambda i: (0,) * a.ndim)
    tab = pl.BlockSpec((tm, RET_QK_WIDTH),
                       lambda i: (jnp.where(i < n_lat_tiles, i % tiles_per_batch, tiles_per_batch), 0))
    chunk_w = S5_CHUNK * S5_GROUP_CH
    token_major = lambda w, dt: (tok(w), jax.ShapeDtypeStruct((t_all, w), dt))
    outs = [token_major(NA_WIDTH, MXU_DTYPE), token_major(NA_WIDTH, MXU_DTYPE), token_major(NA_WIDTH, MXU_DTYPE),
            token_major(RET_QK_WIDTH, MXU_DTYPE), token_major(RET_QK_WIDTH, MXU_DTYPE),
            token_major(RET_V_WIDTH, MXU_DTYPE), token_major(RET_V_WIDTH, F32),
            (pl.BlockSpec((S5_GROUPS, tm // S5_CHUNK, chunk_w), lambda i: (0, i, 0)),
             jax.ShapeDtypeStruct((S5_GROUPS, t_all // S5_CHUNK, chunk_w), F32))]
    return pl.pallas_call(
        functools.partial(_inproj_kernel, tiles_per_batch=tiles_per_batch),
        grid=(t_all // tm,),
        in_specs=[tok(d), full(mods_l), _resident(w_in.shape, layer),
                  full(head_sum), full(q_gain), full(k_gain), tab, tab],
        out_specs=[o[0] for o in outs],
        out_shape=[o[1] for o in outs],
        scratch_shapes=[_token_scratch(tm)],
        compiler_params=_cparams("arbitrary"),
        name="inproj",
    )(x_all, mods_l, w_in, head_sum, q_gain, k_gain, cos_tab, sin_tab)


def _softmax_heads(q, parts):
    lane = lax.broadcasted_iota(jnp.int32, (1, NA_WIDTH), 1)
    acc = jnp.zeros((q.shape[0], NA_WIDTH), F32)
    ones_lane = lambda h: 0 if h != 0 else NA_HEAD_DIM
    v_ones = {ln: [jnp.where(lane == ln, jnp.ones_like(v), v) for _, v, _ in parts] for ln in {ones_lane(h) for h in range(NA_HEADS)}}
    heads = range(NA_HEADS)
    in_head = [(lane // NA_HEAD_DIM) == h for h in heads]
    scores = []
    for h in heads:
        qh = jnp.where(in_head[h], q, jnp.zeros_like(q))
        scores.append([_mxu_nt(qh, k) if bias_fn is None else _mxu_nt(qh, k) + bias_fn(h) for k, _, bias_fn in parts])
    maxes = [functools.reduce(jnp.maximum, [jnp.max(s, axis=-1, keepdims=True) for s in scores[h]]) for h in heads]
    probs = [[jnp.exp2((s - maxes[h]).astype(MXU_DTYPE)) for s in scores[h]] for h in heads]
    outs = [functools.reduce(jnp.add, [_mxu(p, v) for p, v in zip(probs[h], v_ones[ones_lane(h)])]) for h in heads]
    for h in heads:
        ln = ones_lane(h)
        acc = acc + outs[h] * jnp.where(in_head[h], 1.0 / outs[h][:, ln:ln + 1], 0.0)
    return acc


def _na_kernel(q_ref, k0_ref, k1_ref, k2_ref, v0_ref, v1_ref, v2_ref, kc_ref, vc_ref, bias_ref, o_ref, *, nblk):
    i = pl.program_id(1)
    q = q_ref[...]
    kc = kc_ref[...]
    vc = vc_ref[...]

    @pl.when(i < nblk)
    def _():
        k = jnp.concatenate([k0_ref[...], k1_ref[...], k2_ref[...]], axis=0)
        v = jnp.concatenate([v0_ref[...], v1_ref[...], v2_ref[...]], axis=0)
        out = _softmax_heads(q, [(k, v, lambda h: bias_ref[h]), (kc, vc, None)])
        o_ref[...] = out.astype(o_ref.dtype)

    @pl.when(i == nblk)
    def _():
        o_ref[...] = _softmax_heads(q, [(kc, vc, None)]).astype(o_ref.dtype)


def _na_bias_tables(rpb, rows):
    depth = rpb.shape[0]
    nc = 2 * NA_COLS - 1
    lead = GRID_W - NA_COLS
    padded = jnp.pad(rpb.astype(F32) * LOG2_E, ((0, 0),) * 3 + ((lead, 2 * GRID_W - lead - nc),))
    flat = jnp.tile(padded, (1, 1, 1, GRID_W))[..., :GRID_W * (2 * GRID_W - 1)]
    col = flat.reshape(rpb.shape[:3] + (GRID_W, 2 * GRID_W - 1))[..., GRID_W - 1:]
    n_key_rows = 3 * NA_BLOCK_ROWS
    per_a = [col[:, :, NA_ROWS - 1 - NA_BLOCK_ROWS - a:][:, :, :n_key_rows] for a in range(NA_BLOCK_ROWS)]
    base = jnp.transpose(jnp.stack(per_a, axis=2), (0, 1, 2, 4, 3, 5))
    base = base.reshape(depth, NA_HEADS, NA_BLOCK, n_key_rows * GRID_W)
    w = np.arange(GRID_W)
    kc = np.arange(GRID_W)
    valid = np.zeros((3, NA_BLOCK, n_key_rows * GRID_W), bool)
    cs = np.clip(w - NA_COLS // 2, 0, GRID_W - NA_COLS)
    col_ok = (kc[None, :] >= cs[:, None]) & (kc[None, :] < cs[:, None] + NA_COLS)
    for var, r0 in enumerate((0, 2 * NA_BLOCK_ROWS, rows - NA_BLOCK_ROWS)):
        for a in range(NA_BLOCK_ROWS):
            r = r0 + a
            rs = np.clip(r - NA_ROWS // 2, 0, rows - NA_ROWS)
            kr = r0 - NA_BLOCK_ROWS + np.arange(n_key_rows)
            row_ok = (kr >= 0) & (kr < rows) & (kr >= rs) & (kr < rs + NA_ROWS)
            ok = row_ok[None, :, None] & col_ok[:, None, :]
            valid[var, a * GRID_W:(a + 1) * GRID_W] = ok.reshape(GRID_W, -1)
    return jnp.where(jnp.asarray(valid)[None, :, None], base[:, None], MASK_VALUE)


def _neighborhood_attention(qa, ka, va, bias, layer, batch, n_lat):
    t_all = qa.shape[0]
    nblk = n_lat // NA_BLOCK
    ctx0 = batch * nblk
    blk = (NA_BLOCK, NA_WIDTH)

    def qmap(b, i):
        return (jnp.where(i < nblk, b * nblk + i, ctx0 + b), 0)

    kspec = lambda off: pl.BlockSpec(blk, lambda b, i: (b * nblk + jnp.clip(i + off, 0, nblk - 1), 0))
    cspec = pl.BlockSpec(blk, lambda b, i: (ctx0 + b, 0))
    bmap = lambda b, i: (layer, jnp.where(i == 0, 0, jnp.where(i >= nblk - 1, 2, 1)), 0, 0, 0)
    return pl.pallas_call(
        functools.partial(_na_kernel, nblk=nblk),
        grid=(batch, nblk + 1),
        in_specs=[pl.BlockSpec(blk, qmap),
                  kspec(-1), kspec(0), kspec(1), kspec(-1), kspec(0), kspec(1), cspec, cspec,
                  pl.BlockSpec((None, None) + bias.shape[2:], bmap)],
        out_specs=pl.BlockSpec(blk, qmap),
        out_shape=jax.ShapeDtypeStruct((t_all, NA_WIDTH), MXU_DTYPE),
        compiler_params=_cparams("arbitrary", "arbitrary"),
        name="natten",
    )(qa, ka, ka, ka, va, va, va, ka, va, bias)


def _ret_next_state(state, k, v, w_end, g_chunk, block_mask):
    return g_chunk * state + _mxu_tn(k.astype(F32) * w_end, v) * block_mask


def _ret_sweep(chunk_fn, state_ref, order):
    b = pl.program_id(0)
    i = pl.program_id(1)

    @pl.when(i == 0)
    def _():
        zero = jnp.zeros(state_ref.shape, F32)
        finals = [chunk_fn(j, zero) for j in range(len(order))]
        carried = finals[0]
        for j in range(1, len(order)):
            carried = jnp.where(b == j, finals[j], carried)
        state_ref[...] = carried

    @pl.when(i > 0)
    def _():
        state = state_ref[...]
        for j in order:
            state = chunk_fn(j, state)
        state_ref[...] = state


def _ret_fwd_kernel(q_ref, k_ref, v_ref, dmat_ref, win_ref, wend_ref, gch_ref, bm_ref, o_ref, state_ref):
    c = RET_CHUNK
    lane = lax.broadcasted_iota(jnp.int32, (1, RET_QK_WIDTH), 1)
    heads = range(RET_HEADS)

    def chunk(j, state):
        rows = slice(j * c, (j + 1) * c)
        q, k, v = q_ref[rows, :], k_ref[rows, :], v_ref[rows, :]
        cross = _mxu(q, state) * win_ref[...]
        scores = [_mxu_nt(jnp.where((lane // RET_QK_DIM) == h, q, jnp.zeros_like(q)), k) for h in heads]
        decayed = [scores[h] * dmat_ref[h] for h in heads]
        for h in heads:
            sl = slice(h * RET_V_DIM, (h + 1) * RET_V_DIM)
            o_ref[rows, sl] = _mxu(decayed[h], v[:, sl]) + cross[:, sl]
        return _ret_next_state(state, k, v, wend_ref[...], gch_ref[...], bm_ref[...])

    _ret_sweep(chunk, state_ref, tuple(range(q_ref.shape[0] // c)))


def _ret_bwd_kernel(q_ref, k_ref, v_ref, of_ref, gate_ref, win_ref, wend_ref, gch_ref, bm_ref, gn_ref,
                    y_ref, state_ref):
    c = RET_CHUNK
    heads = range(RET_HEADS)
    sls = [slice(h * RET_V_DIM, (h + 1) * RET_V_DIM) for h in heads]

    def chunk(j, state):
        rows = slice(j * c, (j + 1) * c)
        o = of_ref[rows, :] + _mxu(q_ref[rows, :], state) * win_ref[...]
        gate = _silu(gate_ref[rows, :])
        gain = gn_ref[...]
        cen = [o[:, sl] - jnp.mean(o[:, sl], axis=-1, keepdims=True) for sl in sls]
        inv = [lax.rsqrt(jnp.mean(x * x, axis=-1, keepdims=True) + NORM_EPS) for x in cen]
        for h in heads:
            y_ref[rows, sls[h]] = (gate[:, sls[h]] * (cen[h] * inv[h] * gain[:, sls[h]])).astype(y_ref.dtype)
        return _ret_next_state(state, k_ref[rows, :], v_ref[rows, :], wend_ref[...], gch_ref[...], bm_ref[...])

    _ret_sweep(chunk, state_ref, tuple(reversed(range(q_ref.shape[0] // c))))


def _ret_tables(decay):
    c = RET_CHUNK
    log_g = jax.nn.log_sigmoid(decay.astype(F32))
    pos = jnp.arange(c, dtype=F32)
    diff = pos[:, None] - pos[None, :]
    lgf = log_g[0][:, None, None]
    lgb = log_g[1][:, None, None]
    dmat = jnp.where(diff >= 0, jnp.exp(lgf * jnp.maximum(diff, 0.0)), jnp.exp(lgb * jnp.maximum(-diff, 0.0)))
    rep = lambda a, w: jnp.repeat(a, w, axis=-1)
    win_f = rep(jnp.exp(log_g[0][None, :] * (pos[:, None] + 1.0)), RET_V_DIM)
    wend_f = rep(jnp.exp(log_g[0][None, :] * (c - 1.0 - pos[:, None])), RET_QK_DIM)
    win_b = rep(jnp.exp(log_g[1][None, :] * (c - pos[:, None])), RET_V_DIM)
    wend_b = rep(jnp.exp(log_g[1][None, :] * pos[:, None]), RET_QK_DIM)
    gch_f = rep(jnp.exp(log_g[0] * c)[None, :], RET_V_DIM)
    gch_b = rep(jnp.exp(log_g[1] * c)[None, :], RET_V_DIM)
    return dmat, (win_f, wend_f, gch_f), (win_b, wend_b, gch_b)


def _retention(qb, kb, vb, gb, tables, gn, layer, batch, n_lat):
    t_all = qb.shape[0]
    c = batch * RET_CHUNK
    nch = n_lat // c
    ctx0 = batch * nch
    dmat, (win_f, wend_f, gch_f), (win_b, wend_b, gch_b) = tables
    bm = (np.arange(RET_QK_WIDTH)[:, None] // RET_QK_DIM == np.arange(RET_V_WIDTH)[None, :] // RET_V_DIM)
    bm = jnp.asarray(bm, F32)
    fmap = lambda b, i: (jnp.where(i == 0, ctx0, b * nch + i - 1), 0)
    rmap = lambda b, i: (jnp.where(i == 0, ctx0, b * nch + nch - i), 0)
    spare = lambda m: (lambda b, i: (jnp.where(i == 0, ctx0 + b, m(b, i)[0]), 0))
    out_rows = t_all + (batch - 1) * c

    def full(a):
        if a.ndim == 2:
            return pl.BlockSpec(a.shape, lambda b, i: (0, 0))
        return pl.BlockSpec((None,) + a.shape[1:], lambda b, i: (layer,) + (0,) * (a.ndim - 1))

    spec = lambda w, m: pl.BlockSpec((c, w), m)
    state = pltpu.VMEM((RET_QK_WIDTH, RET_V_WIDTH), F32)
    o_f = pl.pallas_call(
        _ret_fwd_kernel,
        grid=(batch, nch + 1),
        in_specs=[spec(RET_QK_WIDTH, fmap), spec(RET_QK_WIDTH, fmap), spec(RET_V_WIDTH, fmap),
                  full(dmat), full(win_f), full(wend_f), full(gch_f), full(bm)],
        out_specs=spec(RET_V_WIDTH, spare(fmap)),
        out_shape=jax.ShapeDtypeStruct((out_rows, RET_V_WIDTH), F32),
        scratch_shapes=[state],
        compiler_params=_cparams("arbitrary", "arbitrary"),
        name="retention_fwd",
    )(qb, kb, vb, dmat, win_f, wend_f, gch_f, bm)
    return pl.pallas_call(
        _ret_bwd_kernel,
        grid=(batch, nch + 1),
        in_specs=[spec(RET_QK_WIDTH, rmap), spec(RET_QK_WIDTH, rmap), spec(RET_V_WIDTH, rmap),
                  spec(RET_V_WIDTH, rmap), spec(RET_V_WIDTH, rmap),
                  full(win_b), full(wend_b), full(gch_b), full(bm), full(gn)],
        out_specs=spec(RET_V_WIDTH, spare(rmap)),
        out_shape=jax.ShapeDtypeStruct((out_rows, RET_V_WIDTH), MXU_DTYPE),
        scratch_shapes=[state],
        compiler_params=_cparams("arbitrary", "arbitrary"),
        name="retention_bwd",
    )(qb, kb, vb, o_f, gb, win_b, wend_b, gch_b, bm, gn)


def _s5_discretize(a_re, a_im, log_dt, b_re, b_im):
    a_re = jnp.minimum(a_re.astype(F32), -1e-4)
    a_im = a_im.astype(F32)
    dt = jnp.exp(log_dt.astype(F32))[..., None]
    mag = jnp.exp(dt * a_re)
    ab_re = mag * jnp.cos(dt * a_im)
    ab_im = mag * jnp.sin(dt * a_im)
    den = a_re * a_re + a_im * a_im
    nr = ab_re - 1.0
    f_re = ((nr * a_re + ab_im * a_im) / den)[..., None]
    f_im = ((ab_im * a_re - nr * a_im) / den)[..., None]
    br = b_re.astype(F32)[None]
    bi = b_im.astype(F32)[None]
    return ab_re, ab_im, f_re * br - f_im * bi, f_re * bi + f_im * br


def _complex_powers(ar, ai, n):
    pr = jnp.ones((1,) + ar.shape, F32)
    pi = jnp.zeros((1,) + ar.shape, F32)
    cr, ci = ar, ai
    while pr.shape[0] < n:
        pr, pi = (jnp.concatenate([pr, pr * cr - pi * ci], axis=0),
                  jnp.concatenate([pi, pr * ci + pi * cr], axis=0))
        cr, ci = cr * cr - ci * ci, 2.0 * cr * ci
    return pr[:n], pi[:n]


def _s5_operators(a_re, a_im, log_dt, b_re, b_im, c_re, c_im, levels):
    G, P, Cg, L = S5_GROUPS, S5_STATE, S5_GROUP_CH, S5_CHUNK
    ab_re, ab_im, bb_re, bb_im = _s5_discretize(a_re, a_im, log_dt, b_re, b_im)
    pr, pi = _complex_powers(ab_re, ab_im, L + 1)
    cr = c_re.astype(F32)
    ci = c_im.astype(F32)
    bt = jnp.concatenate([jnp.swapaxes(bb_re, 2, 3), jnp.swapaxes(bb_im, 2, 3)], axis=3)

    def powers(n, lo, descending):
        sel = lambda z: jnp.flip(z[lo:lo + L, n], axis=0) if descending else z[lo:lo + L, n]
        return sel(pr), sel(pi)

    def state_in(n, lo, descending):
        qr, qi = powers(n, lo, descending)
        wr = jnp.einsum('sgp,gpc->gscp', qr, bb_re[n]) - jnp.einsum('sgp,gpc->gscp', qi, bb_im[n])
        wi = jnp.einsum('sgp,gpc->gscp', qr, bb_im[n]) + jnp.einsum('sgp,gpc->gscp', qi, bb_re[n])
        return jnp.concatenate([wr, wi], axis=-1).reshape(G, L * Cg, 2 * P)

    def state_out(n, lo, descending):
        qr, qi = powers(n, lo, descending)
        er = jnp.einsum('tgp,gcp->gptc', qr, cr[n]) - jnp.einsum('tgp,gcp->gptc', qi, ci[n])
        ei = jnp.einsum('tgp,gcp->gptc', qr, ci[n]) + jnp.einsum('tgp,gcp->gptc', qi, cr[n])
        return jnp.concatenate([er, -ei], axis=1).reshape(G, 2 * P, L * Cg)

    ca = jnp.stack([state_out(0, 0, False), state_out(1, 0, True)])
    f = jnp.concatenate([state_in(0, 0, True), state_in(1, 0, False)], axis=-1).astype(MXU_DTYPE)
    e = jnp.stack([state_out(0, 1, False), state_out(1, 1, True)]).astype(MXU_DTYPE)
    return bt, ca, f, e, _s5_scan_mults(pr[L], pi[L], levels)


def _s5_scan_mults(alr, ali, levels):
    rows = []
    cr, ci = alr, ali
    for _ in range(levels):
        rows.append(jnp.stack([jnp.concatenate([cr, cr], -1), jnp.concatenate([-ci, ci], -1)], axis=-2))
        cr, ci = cr * cr - ci * ci, 2.0 * cr * ci
    return jnp.stack(rows, axis=2)


def _shift_rows(x, k, down):
    n = x.shape[0]
    row = lax.broadcasted_iota(jnp.int32, (n, 1), 0)
    if down:
        return jnp.where(row >= k, pltpu.roll(x, k, 0), 0.0)
    return jnp.where(row < n - k, pltpu.roll(x, n - k, 0), 0.0)


def _swap_halves(x):
    return pltpu.roll(x, S5_STATE, 1)


def _s5_scans(items):
    n = items[0][0].shape[0]
    row = lax.broadcasted_iota(jnp.int32, (n, 1), 0)
    first = [(row == (0 if fwd else n - 1)).astype(F32) for *_, fwd in items]
    times = lambda m_ref, lvl, z: m_ref[lvl, 0:1] * z + m_ref[lvl, 1:2] * _swap_halves(z)
    xs = [loc + f * times(m_ref, 0, init) for (loc, m_ref, init, _), f in zip(items, first)]
    k, lvl = 1, 0
    while k < n:
        shifted = [_shift_rows(x, k, it[3]) for x, it in zip(xs, items)]
        xs = [x + times(it[1], lvl, sx) for x, sx, it in zip(xs, shifted, items)]
        k, lvl = 2 * k, lvl + 1
    out = []
    for x, it, f in zip(xs, items, first):
        last = slice(n - 1, n) if it[3] else slice(0, 1)
        out.append((_shift_rows(x, 1, it[3]) + f * it[2], x[last]))
    return out


def _s5_toeplitz(bt_ref, ca_ref, conv_ref):
    L, Cg = S5_CHUNK, S5_GROUP_CH
    width = L * Cg
    hi = lax.Precision.HIGHEST
    kf = jnp.dot(bt_ref[0], ca_ref[0], preferred_element_type=F32, precision=hi)
    kb = jnp.dot(bt_ref[1], ca_ref[1], preferred_element_type=F32, precision=hi)
    lane = lax.broadcasted_iota(jnp.int32, (Cg, width), 1)
    for s in range(L):
        right = s * Cg
        left = (L - 1 - s) * Cg
        fwd = kf if right == 0 else jnp.where(lane >= right, pltpu.roll(kf, right, 1), 0.0)
        bwd = kb if left == 0 else jnp.where(lane < width - left, pltpu.roll(kb, width - left, 1), 0.0)
        conv_ref[s * Cg:(s + 1) * Cg, :] = fwd + bwd


def _s5_kernel(u_ref, bt_ref, ca_ref, fcat_ref, e_ref, mult_ref, d_ref, y_ref, conv_ref, *, batch, n_lat_ch, n_ctx_ch):
    P2 = 2 * S5_STATE
    _s5_toeplitz(bt_ref, ca_ref, conv_ref)
    u = u_ref[...]
    ub = u.astype(MXU_DTYPE)
    y_ref[...] = _mxu(ub, conv_ref[...]) + u * d_ref[...]
    loc = _mxu(ub, fcat_ref[...])
    zero = jnp.zeros((1, P2), F32)
    chains = [(b, n) for b in range(batch) for n in range(2)]
    ctx_rows = lambda b: slice(batch * n_lat_ch + b * n_ctx_ch, batch * n_lat_ch + (b + 1) * n_ctx_ch)
    lat_rows = lambda b: slice(b * n_lat_ch, (b + 1) * n_lat_ch)
    item = lambda rows, n, init: (loc[rows, n * P2:(n + 1) * P2], mult_ref.at[n], init, n == 0)
    ctx_out = _s5_scans([item(ctx_rows(b), n, zero) for b, n in chains])
    lat_out = _s5_scans([item(lat_rows(b), n, x_last) for (b, n), (_, x_last) in zip(chains, ctx_out)])
    for (b, n), (prev_c, _), (prev_l, _) in zip(chains, ctx_out, lat_out):
        y_ref[ctx_rows(b), :] += _mxu(prev_c, e_ref[n])
        y_ref[lat_rows(b), :] += _mxu(prev_l, e_ref[n])


def _s5_mixer(ug, ops, d_skip, layer, batch, n_lat, n_ctx):
    bt, ca, f_cat, e, mults = ops
    G, Cg, L, P2 = S5_GROUPS, S5_GROUP_CH, S5_CHUNK, 2 * S5_STATE
    rows, width = ug.shape[1], L * Cg
    levels = mults.shape[3]
    per_g = lambda *s: pl.BlockSpec((None,) + s, lambda g: (g,) + (0,) * len(s))
    per_dir = lambda *s: pl.BlockSpec((None, 2, None) + s, lambda g: (layer, 0, g) + (0,) * len(s))
    return pl.pallas_call(
        functools.partial(_s5_kernel, batch=batch, n_lat_ch=n_lat // L, n_ctx_ch=n_ctx // L),
        grid=(G,),
        in_specs=[per_g(rows, width), per_dir(Cg, P2), per_dir(P2, width),
                  pl.BlockSpec((None, None, width, 2 * P2), lambda g: (layer, g, 0, 0)),
                  per_dir(P2, width), per_dir(levels, 2, P2),
                  pl.BlockSpec((None, None, 1, width), lambda g: (layer, g, 0, 0))],
        out_specs=per_g(rows, width),
        out_shape=jax.ShapeDtypeStruct((G, rows, width), F32),
        scratch_shapes=[pltpu.VMEM((width, width), F32)],
        compiler_params=_cparams("arbitrary"),
        name="s5_conv",
    )(ug, bt, ca, f_cat, e, mults, d_skip)


def _mix_ffn_kernel(x_ref, mod_ref, ya_ref, yb_ref, yg_ref, wglu_ref, wo_ref, win_ref, wout_ref, o_ref, tok_ref,
                    *, tiles_per_batch):
    d = x_ref.shape[1]
    row = pl.program_id(0) // tiles_per_batch
    _groups_to_tokens(yg_ref, tok_ref)
    yc = jnp.concatenate([tok_ref[hb] for hb in range(S5_WIDTH // LANES)], axis=1)
    glu = _mxu(jax.nn.gelu(yc), wglu_ref[...])
    s5 = glu[:, :S5_WIDTH] * jax.nn.sigmoid(glu[:, S5_WIDTH:])
    mix = (_mxu(ya_ref[...], wo_ref[0:NA_WIDTH, :])
           + _mxu(yb_ref[...], wo_ref[NA_WIDTH:NA_WIDTH + RET_V_WIDTH, :])
           + _mxu(s5, wo_ref[NA_WIDTH + RET_V_WIDTH:, :]))
    x = x_ref[...] + _mod_rows(mod_ref, row, 5, d) * mix
    o_ref[...] = _ffn_body(x, mod_ref, row, 6, win_ref, wout_ref)


def _mix_ffn(x_all, mods_l, ya, yb, yg, w_glu, w_out, w_in2, w_out2, layer, tiles_per_batch, n_tiles):
    d = x_all.shape[1]
    tm = TOKEN_TILE
    tok = lambda w: pl.BlockSpec((tm, w), lambda i: (i, 0))
    return pl.pallas_call(
        functools.partial(_mix_ffn_kernel, tiles_per_batch=tiles_per_batch),
        grid=(n_tiles,),
        in_specs=[tok(d), pl.BlockSpec(mods_l.shape, lambda i: (0, 0)), tok(NA_WIDTH), tok(RET_V_WIDTH),
                  pl.BlockSpec((S5_GROUPS, tm // S5_CHUNK, yg.shape[2]), lambda i: (0, i, 0)),
                  _resident(w_glu.shape, layer), _resident(w_out.shape, layer),
                  _resident(w_in2.shape, layer), _resident(w_out2.shape, layer)],
        out_specs=tok(d),
        out_shape=jax.ShapeDtypeStruct((n_tiles * tm, d), F32),
        scratch_shapes=[_token_scratch(tm)],
        compiler_params=_cparams("arbitrary"),
        name="mix_ffn2",
    )(x_all, mods_l, ya, yb, yg, w_glu, w_out, w_in2, w_out2)


def _rope_tables(n_lat, n_ctx):
    nf = RET_QK_DIM // 4
    inv = ROPE_BASE ** (-jnp.arange(nf, dtype=F32) / nf)
    t = jnp.arange(n_lat)
    row = (t // GRID_W).astype(F32)
    col = (t % GRID_W).astype(F32)
    ang = jnp.concatenate([row[:, None] * inv, col[:, None] * inv], axis=-1)
    cos, sin = jnp.cos(ang), jnp.sin(ang)
    cos_l = jnp.tile(jnp.concatenate([cos, cos], axis=-1), (1, RET_HEADS))
    sin_l = jnp.tile(jnp.concatenate([-sin, sin], axis=-1), (1, RET_HEADS))
    pad = TOKEN_TILE
    return (jnp.concatenate([cos_l, jnp.ones((pad, RET_QK_WIDTH), F32)], axis=0),
            jnp.concatenate([sin_l, jnp.zeros((pad, RET_QK_WIDTH), F32)], axis=0))


def kernel(x, c, ctx, c_ctx, w_mod, b_mod, ffn1_w_in, ffn1_w_out, w_in, w_out, na_q_gain, na_k_gain, na_rpb,
           ret_decay, ret_gn, s5_a_re, s5_a_im, s5_log_dt, s5_b_re, s5_b_im, s5_c_re, s5_c_im, s5_d, s5_w_glu,
           ffn2_w_in, ffn2_w_out):
    batch, n_lat, d = x.shape
    n_ctx = ctx.shape[1]
    depth = w_mod.shape[0]
    assert n_ctx == NA_BLOCK == RET_CHUNK and batch * n_ctx == TOKEN_TILE
    assert n_lat % TOKEN_TILE == 0 and n_lat // NA_BLOCK >= 4
    assert batch + 1 <= 8
    tiles_per_batch = n_lat // TOKEN_TILE
    n_lat_tiles = batch * tiles_per_batch
    rows = n_lat // GRID_W

    bf = lambda a: a.astype(MXU_DTYPE)
    ffn1_w_in, ffn1_w_out, ffn2_w_in, ffn2_w_out = bf(ffn1_w_in), bf(ffn1_w_out), bf(ffn2_w_in), bf(ffn2_w_out)
    w_in, w_out, s5_w_glu = bf(w_in), bf(w_out), bf(s5_w_glu)

    cond = jnp.concatenate([c, c_ctx[None, :], jnp.zeros((8 - batch - 1, d), F32)], axis=0)
    mods = _modulation(cond, w_mod, b_mod)

    head_sum = jnp.asarray(np.arange(NA_WIDTH)[:, None] // NA_HEAD_DIM == np.arange(NA_WIDTH)[None, :] // NA_HEAD_DIM,
                           MXU_DTYPE)
    cos_tab, sin_tab = _rope_tables(n_lat, n_ctx)
    levels = max(1, int(math.ceil(math.log2(max(n_lat, n_ctx) // S5_CHUNK))))
    s5_ops = jax.vmap(functools.partial(_s5_operators, levels=levels))(
        s5_a_re, s5_a_im, s5_log_dt, s5_b_re, s5_b_im, s5_c_re, s5_c_im)

    na_bias = _na_bias_tables(na_rpb, rows)
    ret_tables = jax.vmap(_ret_tables)(ret_decay)
    ret_gain = ret_gn.astype(F32).reshape(depth, 1, RET_V_WIDTH)
    q_gain = jnp.tile(na_q_gain.astype(F32), (1, NA_HEADS)).reshape(depth, 1, NA_WIDTH)
    k_gain = jnp.tile(na_k_gain.astype(F32), (1, NA_HEADS)).reshape(depth, 1, NA_WIDTH)
    s5_skip = jnp.tile(s5_d.astype(F32).reshape(depth, S5_GROUPS, 1, S5_GROUP_CH), (1, 1, 1, S5_CHUNK))

    x_all = x.reshape(batch * n_lat, d)
    x_ctx = ctx.reshape(batch * n_ctx, d)
    for l in range(depth):
        mods_l = mods[l]
        x_all = _ffn_first(x_all, x_ctx if l == 0 else None, mods_l, ffn1_w_in, ffn1_w_out, l, tiles_per_batch)
        qa, ka, va, qb, kb, vb, gb, ug = _inproj(x_all, mods_l, w_in, l, head_sum, q_gain[l], k_gain[l],
                                                 cos_tab, sin_tab, tiles_per_batch, n_lat_tiles)
        ya = _neighborhood_attention(qa, ka, va, na_bias, l, batch, n_lat)
        yb = _retention(qb, kb, vb, gb, ret_tables, ret_gain, l, batch, n_lat)
        yg = _s5_mixer(ug, s5_ops, s5_skip, l, batch, n_lat, n_ctx)
        n_tiles = n_lat_tiles + 1 if l < depth - 1 else n_lat_tiles
        x_all = _mix_ffn(x_all, mods_l, ya, yb, yg, s5_w_glu, w_out, ffn2_w_in, ffn2_w_out, l, tiles_per_batch, n_tiles)
    return x_all.reshape(batch, n_lat, d)
```

```python
import functools
import math

import numpy as np
import jax
import jax.numpy as jnp
from jax import lax
from jax.experimental import pallas as pl
from jax.experimental.pallas import tpu as pltpu

F32 = jnp.float32
MXU_DTYPE = jnp.bfloat16

GRID_W = 64
N_MOD = 9
NORM_EPS = 1e-6
NA_HEADS, NA_HEAD_DIM, NA_ROWS, NA_COLS = 8, 32, 8, 16
RET_HEADS, RET_QK_DIM, RET_V_DIM = 4, 64, 128
ROPE_BASE = 10000.0
S5_GROUPS, S5_GROUP_CH, S5_STATE = 16, 16, 64
NA_WIDTH = NA_HEADS * NA_HEAD_DIM
RET_QK_WIDTH = RET_HEADS * RET_QK_DIM
RET_V_WIDTH = RET_HEADS * RET_V_DIM
S5_WIDTH = S5_GROUPS * S5_GROUP_CH

TOKEN_TILE = 512
FFN_CHUNK = 256
NA_BLOCK_ROWS = 4
NA_BLOCK = NA_BLOCK_ROWS * GRID_W
RET_CHUNK = 256
S5_CHUNK = 32
MASK_VALUE = -1e30
LOG2_E = math.log2(math.e)
LANES = 128
VMEM_LIMIT = 56 * 1024 * 1024


def _mxu(a, b):
    return jnp.dot(a.astype(MXU_DTYPE), b.astype(MXU_DTYPE), preferred_element_type=F32)


def _mxu_nt(a, b):
    return lax.dot_general(a.astype(MXU_DTYPE), b.astype(MXU_DTYPE), (((1,), (1,)), ((), ())),
                           preferred_element_type=F32)


def _mxu_tn(a, b):
    return lax.dot_general(a.astype(MXU_DTYPE), b.astype(MXU_DTYPE), (((0,), (0,)), ((), ())),
                           preferred_element_type=F32)


def _silu(x):
    return x * jax.nn.sigmoid(x)


def _cparams(*sem):
    return pltpu.CompilerParams(dimension_semantics=sem, vmem_limit_bytes=VMEM_LIMIT)


def _mod_kernel(s_ref, w_ref, b_ref, o_ref):
    s = _silu(s_ref[...])
    o_ref[...] = _mxu(s, w_ref[...]) + b_ref[...]


def _modulation(cond_rows, w_mod, b_mod):
    depth, d, nd = w_mod.shape
    tn = 1024
    return pl.pallas_call(
        _mod_kernel,
        grid=(depth, nd // tn),
        in_specs=[pl.BlockSpec((8, d), lambda l, j: (0, 0)),
                  pl.BlockSpec((None, d, tn), lambda l, j: (l, 0, j)),
                  pl.BlockSpec((None, 1, tn), lambda l, j: (l, 0, j))],
        out_specs=pl.BlockSpec((None, 8, tn), lambda l, j: (l, 0, j)),
        out_shape=jax.ShapeDtypeStruct((depth, 8, nd), F32),
        compiler_params=_cparams("arbitrary", "arbitrary"),
        name="modulation",
    )(cond_rows, w_mod, b_mod.reshape(depth, 1, nd))


def _mod_rows(mod_ref, row, j, d):
    return mod_ref[pl.ds(row, 1), j * d:(j + 1) * d]


def _norm_mod(x, shift, scale):
    ms = jnp.mean(x * x, axis=-1, keepdims=True)
    return (x * lax.rsqrt(ms + NORM_EPS)) * (1.0 + scale) + shift


def _ffn_body(x, mod_ref, row, j0, win_ref, wout_ref):
    d = x.shape[1]
    dff = wout_ref.shape[0]
    h = _norm_mod(x, _mod_rows(mod_ref, row, j0, d), _mod_rows(mod_ref, row, j0 + 1, d)).astype(MXU_DTYPE)
    acc = jnp.zeros(x.shape, F32)
    for c in range(dff // FFN_CHUNK):
        lo = c * FFN_CHUNK
        a = _mxu(h, win_ref[:, lo:lo + FFN_CHUNK])
        b = _mxu(h, win_ref[:, dff + lo:dff + lo + FFN_CHUNK])
        acc = acc + _mxu(_silu(a) * b, wout_ref[lo:lo + FFN_CHUNK, :])
    return x + (0.5 * _mod_rows(mod_ref, row, j0 + 2, d)) * acc


def _ffn_kernel(x_ref, mod_ref, win_ref, wout_ref, o_ref, *, tiles_per_batch):
    row = pl.program_id(0) // tiles_per_batch
    o_ref[...] = _ffn_body(x_ref[...], mod_ref, row, 0, win_ref, wout_ref)


def _ffn_split_kernel(x_ref, ctx_ref, mod_ref, win_ref, wout_ref, o_ref, *, tiles_per_batch, n_lat_tiles):
    i = pl.program_id(0)
    x = jnp.where(i < n_lat_tiles, x_ref[...], ctx_ref[...])
    o_ref[...] = _ffn_body(x, mod_ref, i // tiles_per_batch, 0, win_ref, wout_ref)


def _resident(shape, layer):
    return pl.BlockSpec((None,) + tuple(shape[1:]), lambda i: (layer,) + (0,) * (len(shape) - 1),
                        pipeline_mode=pl.Buffered(1))


def _ffn_first(x_lat, x_ctx, mods_l, w_in, w_out, layer, tiles_per_batch):
    d = x_lat.shape[1]
    tm = TOKEN_TILE
    tok = pl.BlockSpec((tm, d), lambda i: (i, 0))
    mod = pl.BlockSpec(mods_l.shape, lambda i: (0, 0))
    weights = [_resident(w_in.shape, layer), _resident(w_out.shape, layer)]
    if x_ctx is None:
        t_all = x_lat.shape[0]
        kern = functools.partial(_ffn_kernel, tiles_per_batch=tiles_per_batch)
        in_specs, args = [tok, mod] + weights, (x_lat, mods_l, w_in, w_out)
    else:
        n_lat_tiles = x_lat.shape[0] // tm
        t_all = x_lat.shape[0] + x_ctx.shape[0]
        kern = functools.partial(_ffn_split_kernel, tiles_per_batch=tiles_per_batch, n_lat_tiles=n_lat_tiles)
        in_specs = [pl.BlockSpec((tm, d), lambda i: (jnp.minimum(i, n_lat_tiles - 1), 0)),
                    pl.BlockSpec((tm, d), lambda i: (0, 0)), mod] + weights
        args = (x_lat, x_ctx, mods_l, w_in, w_out)
    return pl.pallas_call(
        kern, grid=(t_all // tm,), in_specs=in_specs, out_specs=tok,
        out_shape=jax.ShapeDtypeStruct((t_all, d), F32),
        compiler_params=_cparams("arbitrary"), name="ffn1",
    )(*args)


def _transpose_blocks(v):
    nb = len(v)
    lane_blk = lax.broadcasted_iota(jnp.int32, v[0].shape, 1) // S5_GROUP_CH
    s = nb // 2
    while s >= 1:
        upper = (lane_blk & s) != 0
        out = list(v)
        for a in range(nb):
            if a & s:
                continue
            b = a + s
            out[a] = jnp.where(upper, pltpu.roll(v[b], s * S5_GROUP_CH, 1), v[a])
            out[b] = jnp.where(upper, v[b], pltpu.roll(v[a], LANES - s * S5_GROUP_CH, 1))
        v = out
        s //= 2
    return v


def _token_scratch(tm):
    return pltpu.VMEM((S5_WIDTH // LANES, tm, LANES), F32)


def _tokens_to_groups(tok_ref, ug_ref):
    L, nb = S5_CHUNK, LANES // S5_GROUP_CH
    n = tok_ref.shape[1] // L
    for hb in range(tok_ref.shape[0]):
        for q in range(L // nb):
            by_group = _transpose_blocks([tok_ref[hb, pl.ds(q * nb + r, n, stride=L), :] for r in range(nb)])
            for j in range(nb):
                ug_ref[hb * nb + j, :, q * LANES:(q + 1) * LANES] = by_group[j]


def _groups_to_tokens(yg_ref, tok_ref):
    L, nb = S5_CHUNK, LANES // S5_GROUP_CH
    n = tok_ref.shape[1] // L
    for hb in range(tok_ref.shape[0]):
        for q in range(L // nb):
            by_token = _transpose_blocks([yg_ref[hb * nb + j, :, q * LANES:(q + 1) * LANES] for j in range(nb)])
            for r in range(nb):
                tok_ref[hb, pl.ds(q * nb + r, n, stride=L), :] = by_token[r]


def _head_rms(z, head_sum, gain):
    zz = z * z
    hi = zz.astype(MXU_DTYPE)
    lo = zz - hi.astype(F32)
    ms = (_mxu(hi, head_sum) + _mxu(lo, head_sum)) * (1.0 / NA_HEAD_DIM)
    return z * lax.rsqrt(ms + NORM_EPS) * gain


def _rotate(z, cos, sin_signed):
    tm = z.shape[0]
    lane = lax.broadcasted_iota(jnp.int32, (tm, 128), 1)
    first_half = (lane % RET_QK_DIM) < (RET_QK_DIM // 2)
    outs = []
    for c in range(z.shape[1] // 128):
        zc = z[:, c * 128:(c + 1) * 128]
        partner = jnp.where(first_half, pltpu.roll(zc, 128 - RET_QK_DIM // 2, 1), pltpu.roll(zc, RET_QK_DIM // 2, 1))
        outs.append(zc * cos[:, c * 128:(c + 1) * 128] + partner * sin_signed[:, c * 128:(c + 1) * 128])
    return jnp.concatenate(outs, axis=1)


def _inproj_kernel(x_ref, mod_ref, w_ref, hs_ref, qg_ref, kg_ref, cos_ref, sin_ref,
                   qa_ref, ka_ref, va_ref, qb_ref, kb_ref, vb_ref, gb_ref, ug_ref, tok_ref, *, tiles_per_batch):
    d = x_ref.shape[1]
    row = pl.program_id(0) // tiles_per_batch
    x = x_ref[...]
    h = _norm_mod(x, _mod_rows(mod_ref, row, 3, d), _mod_rows(mod_ref, row, 4, d)).astype(MXU_DTYPE)

    def proj(lo, width):
        return _mxu(h, w_ref[:, lo:lo + width])

    widths = (NA_WIDTH, NA_WIDTH, NA_WIDTH, RET_QK_WIDTH, RET_QK_WIDTH, RET_V_WIDTH, RET_V_WIDTH, S5_WIDTH)
    off = [sum(widths[:i]) for i in range(len(widths))]
    z_u, z_qa, z_ka, z_qb, z_kb = (proj(off[i], widths[i]) for i in (7, 0, 1, 3, 4))
    z_va, z_vb, z_gb = (proj(off[i], widths[i]) for i in (2, 5, 6))
    for hb in range(S5_WIDTH // LANES):
        tok_ref[hb] = z_u[:, hb * LANES:(hb + 1) * LANES]
    _tokens_to_groups(tok_ref, ug_ref)
    head_sum = hs_ref[...]
    qa_ref[...] = (_head_rms(z_qa, head_sum, qg_ref[...]) * (NA_HEAD_DIM ** -0.5 * LOG2_E)).astype(qa_ref.dtype)
    ka_ref[...] = _head_rms(z_ka, head_sum, kg_ref[...]).astype(ka_ref.dtype)
    cos = cos_ref[...]
    sin = sin_ref[...]
    qb_ref[...] = _rotate(z_qb, cos, sin).astype(qb_ref.dtype)
    kb_ref[...] = (_rotate(z_kb, cos, sin) * (RET_QK_DIM ** -0.5)).astype(kb_ref.dtype)
    va_ref[...] = z_va.astype(va_ref.dtype)
    vb_ref[...] = z_vb.astype(vb_ref.dtype)
    gb_ref[...] = z_gb


def _inproj(x_all, mods_l, w_in, layer, head_sum, q_gain, k_gain, cos_tab, sin_tab, tiles_per_batch, n_lat_tiles):
    t_all, d = x_all.shape
    tm = TOKEN_TILE
    tok = lambda w: pl.BlockSpec((tm, w), lambda i: (i, 0))
    full = lambda a: pl.BlockSpec(a.shape, lambda i: (0,) * a.ndim)
    tab = pl.BlockSpec((tm, RET_QK_WIDTH),
                       lambda i: (jnp.where(i < n_lat_tiles, i % tiles_per_batch, tiles_per_batch), 0))
    chunk_w = S5_CHUNK * S5_GROUP_CH
    token_major = lambda w, dt: (tok(w), jax.ShapeDtypeStruct((t_all, w), dt))
    outs = [token_major(NA_WIDTH, MXU_DTYPE), token_major(NA_WIDTH, MXU_DTYPE), token_major(NA_WIDTH, MXU_DTYPE),
            token_major(RET_QK_WIDTH, MXU_DTYPE), token_major(RET_QK_WIDTH, MXU_DTYPE),
            token_major(RET_V_WIDTH, MXU_DTYPE), token_major(RET_V_WIDTH, F32),
            (pl.BlockSpec((S5_GROUPS, tm // S5_CHUNK, chunk_w), lambda i: (0, i, 0)),
             jax.ShapeDtypeStruct((S5_GROUPS, t_all // S5_CHUNK, chunk_w), F32))]
    return pl.pallas_call(
        functools.partial(_inproj_kernel, tiles_per_batch=tiles_per_batch),
        grid=(t_all // tm,),
        in_specs=[tok(d), full(mods_l), _resident(w_in.shape, layer),
                  full(head_sum), full(q_gain), full(k_gain), tab, tab],
        out_specs=[o[0] for o in outs],
        out_shape=[o[1] for o in outs],
        scratch_shapes=[_token_scratch(tm)],
        compiler_params=_cparams("arbitrary"),
        name="inproj",
    )(x_all, mods_l, w_in, head_sum, q_gain, k_gain, cos_tab, sin_tab)


def _softmax_heads(q, parts):
    lane = lax.broadcasted_iota(jnp.int32, (1, NA_WIDTH), 1)
    acc = jnp.zeros((q.shape[0], NA_WIDTH), F32)
    ones_lane = lambda h: 0 if h != 0 else NA_HEAD_DIM
    v_ones = {ln: [jnp.where(lane == ln, jnp.ones_like(v), v) for _, v, _ in parts] for ln in {ones_lane(h) for h in range(NA_HEADS)}}
    heads = range(NA_HEADS)
    in_head = [(lane // NA_HEAD_DIM) == h for h in heads]
    scores = []
    for h in heads:
        qh = jnp.where(in_head[h], q, jnp.zeros_like(q))
        scores.append([_mxu_nt(qh, k) if bias_fn is None else _mxu_nt(qh, k) + bias_fn(h) for k, _, bias_fn in parts])
    maxes = [functools.reduce(jnp.maximum, [jnp.max(s, axis=-1, keepdims=True) for s in scores[h]]) for h in heads]
    probs = [[jnp.exp2((s - maxes[h]).astype(MXU_DTYPE)) for s in scores[h]] for h in heads]
    outs = [functools.reduce(jnp.add, [_mxu(p, v) for p, v in zip(probs[h], v_ones[ones_lane(h)])]) for h in heads]
    for h in heads:
        ln = ones_lane(h)
        acc = acc + outs[h] * jnp.where(in_head[h], 1.0 / outs[h][:, ln:ln + 1], 0.0)
    return acc


def _na_kernel(q_ref, k0_ref, k1_ref, k2_ref, v0_ref, v1_ref, v2_ref, kc_ref, vc_ref, bias_ref, o_ref, *, nblk):
    i = pl.program_id(1)
    q = q_ref[...]
    kc = kc_ref[...]
    vc = vc_ref[...]

    @pl.when(i < nblk)
    def _():
        k = jnp.concatenate([k0_ref[...], k1_ref[...], k2_ref[...]], axis=0)
        v = jnp.concatenate([v0_ref[...], v1_ref[...], v2_ref[...]], axis=0)
        out = _softmax_heads(q, [(k, v, lambda h: bias_ref[h]), (kc, vc, None)])
        o_ref[...] = out.astype(o_ref.dtype)

    @pl.when(i == nblk)
    def _():
        o_ref[...] = _softmax_heads(q, [(kc, vc, None)]).astype(o_ref.dtype)


def _na_bias_tables(rpb, rows):
    depth = rpb.shape[0]
    nc = 2 * NA_COLS - 1
    lead = GRID_W - NA_COLS
    padded = jnp.pad(rpb.astype(F32) * LOG2_E, ((0, 0),) * 3 + ((lead, 2 * GRID_W - lead - nc),))
    flat = jnp.tile(padded, (1, 1, 1, GRID_W))[..., :GRID_W * (2 * GRID_W - 1)]
    col = flat.reshape(rpb.shape[:3] + (GRID_W, 2 * GRID_W - 1))[..., GRID_W - 1:]
    n_key_rows = 3 * NA_BLOCK_ROWS
    per_a = [col[:, :, NA_ROWS - 1 - NA_BLOCK_ROWS - a:][:, :, :n_key_rows] for a in range(NA_BLOCK_ROWS)]
    base = jnp.transpose(jnp.stack(per_a, axis=2), (0, 1, 2, 4, 3, 5))
    base = base.reshape(depth, NA_HEADS, NA_BLOCK, n_key_rows * GRID_W)
    w = np.arange(GRID_W)
    kc = np.arange(GRID_W)
    valid = np.zeros((3, NA_BLOCK, n_key_rows * GRID_W), bool)
    cs = np.clip(w - NA_COLS // 2, 0, GRID_W - NA_COLS)
    col_ok = (kc[None, :] >= cs[:, None]) & (kc[None, :] < cs[:, None] + NA_COLS)
    for var, r0 in enumerate((0, 2 * NA_BLOCK_ROWS, rows - NA_BLOCK_ROWS)):
        for a in range(NA_BLOCK_ROWS):
            r = r0 + a
            rs = np.clip(r - NA_ROWS // 2, 0, rows - NA_ROWS)
            kr = r0 - NA_BLOCK_ROWS + np.arange(n_key_rows)
            row_ok = (kr >= 0) & (kr < rows) & (kr >= rs) & (kr < rs + NA_ROWS)
            ok = row_ok[None, :, None] & col_ok[:, None, :]
            valid[var, a * GRID_W:(a + 1) * GRID_W] = ok.reshape(GRID_W, -1)
    return jnp.where(jnp.asarray(valid)[None, :, None], base[:, None], MASK_VALUE)


def _neighborhood_attention(qa, ka, va, bias, layer, batch, n_lat):
    t_all = qa.shape[0]
    nblk = n_lat // NA_BLOCK
    ctx0 = batch * nblk
    blk = (NA_BLOCK, NA_WIDTH)

    def qmap(b, i):
        return (jnp.where(i < nblk, b * nblk + i, ctx0 + b), 0)

    kspec = lambda off: pl.BlockSpec(blk, lambda b, i: (b * nblk + jnp.clip(i + off, 0, nblk - 1), 0))
    cspec = pl.BlockSpec(blk, lambda b, i: (ctx0 + b, 0))
    bmap = lambda b, i: (layer, jnp.where(i == 0, 0, jnp.where(i >= nblk - 1, 2, 1)), 0, 0, 0)
    return pl.pallas_call(
        functools.partial(_na_kernel, nblk=nblk),
        grid=(batch, nblk + 1),
        in_specs=[pl.BlockSpec(blk, qmap),
                  kspec(-1), kspec(0), kspec(1), kspec(-1), kspec(0), kspec(1), cspec, cspec,
                  pl.BlockSpec((None, None) + bias.shape[2:], bmap)],
        out_specs=pl.BlockSpec(blk, qmap),
        out_shape=jax.ShapeDtypeStruct((t_all, NA_WIDTH), MXU_DTYPE),
        compiler_params=_cparams("arbitrary", "arbitrary"),
        name="natten",
    )(qa, ka, ka, ka, va, va, va, ka, va, bias)


def _ret_next_state(state, k, v, w_end, g_chunk, block_mask):
    return g_chunk * state + _mxu_tn(k.astype(F32) * w_end, v) * block_mask


def _ret_sweep(chunk_fn, state_ref, order):
    b = pl.program_id(0)
    i = pl.program_id(1)

    @pl.when(i == 0)
    def _():
        zero = jnp.zeros(state_ref.shape, F32)
        finals = [chunk_fn(j, zero) for j in range(len(order))]
        carried = finals[0]
        for j in range(1, len(order)):
            carried = jnp.where(b == j, finals[j], carried)
        state_ref[...] = carried

    @pl.when(i > 0)
    def _():
        state = state_ref[...]
        for j in order:
            state = chunk_fn(j, state)
        state_ref[...] = state


def _ret_fwd_kernel(q_ref, k_ref, v_ref, dmat_ref, win_ref, wend_ref, gch_ref, bm_ref, o_ref, state_ref):
    c = RET_CHUNK
    lane = lax.broadcasted_iota(jnp.int32, (1, RET_QK_WIDTH), 1)
    heads = range(RET_HEADS)

    def chunk(j, state):
        rows = slice(j * c, (j + 1) * c)
        q, k, v = q_ref[rows, :], k_ref[rows, :], v_ref[rows, :]
        cross = _mxu(q, state) * win_ref[...]
        scores = [_mxu_nt(jnp.where((lane // RET_QK_DIM) == h, q, jnp.zeros_like(q)), k) for h in heads]
        decayed = [scores[h] * dmat_ref[h] for h in heads]
        for h in heads:
            sl = slice(h * RET_V_DIM, (h + 1) * RET_V_DIM)
            o_ref[rows, sl] = _mxu(decayed[h], v[:, sl]) + cross[:, sl]
        return _ret_next_state(state, k, v, wend_ref[...], gch_ref[...], bm_ref[...])

    _ret_sweep(chunk, state_ref, tuple(range(q_ref.shape[0] // c)))


def _ret_bwd_kernel(q_ref, k_ref, v_ref, of_ref, gate_ref, win_ref, wend_ref, gch_ref, bm_ref, gn_ref,
                    y_ref, state_ref):
    c = RET_CHUNK
    heads = range(RET_HEADS)
    sls = [slice(h * RET_V_DIM, (h + 1) * RET_V_DIM) for h in heads]

    def chunk(j, state):
        rows = slice(j * c, (j + 1) * c)
        o = of_ref[rows, :] + _mxu(q_ref[rows, :], state) * win_ref[...]
        gate = _silu(gate_ref[rows, :])
        gain = gn_ref[...]
        cen = [o[:, sl] - jnp.mean(o[:, sl], axis=-1, keepdims=True) for sl in sls]
        inv = [lax.rsqrt(jnp.mean(x * x, axis=-1, keepdims=True) + NORM_EPS) for x in cen]
        for h in heads:
            y_ref[rows, sls[h]] = (gate[:, sls[h]] * (cen[h] * inv[h] * gain[:, sls[h]])).astype(y_ref.dtype)
        return _ret_next_state(state, k_ref[rows, :], v_ref[rows, :], wend_ref[...], gch_ref[...], bm_ref[...])

    _ret_sweep(chunk, state_ref, tuple(reversed(range(q_ref.shape[0] // c))))


def _ret_tables(decay):
    c = RET_CHUNK
    log_g = jax.nn.log_sigmoid(decay.astype(F32))
    pos = jnp.arange(c, dtype=F32)
    diff = pos[:, None] - pos[None, :]
    lgf = log_g[0][:, None, None]
    lgb = log_g[1][:, None, None]
    dmat = jnp.where(diff >= 0, jnp.exp(lgf * jnp.maximum(diff, 0.0)), jnp.exp(lgb * jnp.maximum(-diff, 0.0)))
    rep = lambda a, w: jnp.repeat(a, w, axis=-1)
    win_f = rep(jnp.exp(log_g[0][None, :] * (pos[:, None] + 1.0)), RET_V_DIM)
    wend_f = rep(jnp.exp(log_g[0][None, :] * (c - 1.0 - pos[:, None])), RET_QK_DIM)
    win_b = rep(jnp.exp(log_g[1][None, :] * (c - pos[:, None])), RET_V_DIM)
    wend_b = rep(jnp.exp(log_g[1][None, :] * pos[:, None]), RET_QK_DIM)
    gch_f = rep(jnp.exp(log_g[0] * c)[None, :], RET_V_DIM)
    gch_b = rep(jnp.exp(log_g[1] * c)[None, :], RET_V_DIM)
    return dmat, (win_f, wend_f, gch_f), (win_b, wend_b, gch_b)


def _retention(qb, kb, vb, gb, tables, gn, layer, batch, n_lat):
    t_all = qb.shape[0]
    c = batch * RET_CHUNK
    nch = n_lat // c
    ctx0 = batch * nch
    dmat, (win_f, wend_f, gch_f), (win_b, wend_b, gch_b) = tables
    bm = (np.arange(RET_QK_WIDTH)[:, None] // RET_QK_DIM == np.arange(RET_V_WIDTH)[None, :] // RET_V_DIM)
    bm = jnp.asarray(bm, F32)
    fmap = lambda b, i: (jnp.where(i == 0, ctx0, b * nch + i - 1), 0)
    rmap = lambda b, i: (jnp.where(i == 0, ctx0, b * nch + nch - i), 0)
    spare = lambda m: (lambda b, i: (jnp.where(i == 0, ctx0 + b, m(b, i)[0]), 0))
    out_rows = t_all + (batch - 1) * c

    def full(a):
        if a.ndim == 2:
            return pl.BlockSpec(a.shape, lambda b, i: (0, 0))
        return pl.BlockSpec((None,) + a.shape[1:], lambda b, i: (layer,) + (0,) * (a.ndim - 1))

    spec = lambda w, m: pl.BlockSpec((c, w), m)
    state = pltpu.VMEM((RET_QK_WIDTH, RET_V_WIDTH), F32)
    o_f = pl.pallas_call(
        _ret_fwd_kernel,
        grid=(batch, nch + 1),
        in_specs=[spec(RET_QK_WIDTH, fmap), spec(RET_QK_WIDTH, fmap), spec(RET_V_WIDTH, fmap),
                  full(dmat), full(win_f), full(wend_f), full(gch_f), full(bm)],
        out_specs=spec(RET_V_WIDTH, spare(fmap)),
        out_shape=jax.ShapeDtypeStruct((out_rows, RET_V_WIDTH), F32),
        scratch_shapes=[state],
        compiler_params=_cparams("arbitrary", "arbitrary"),
        name="retention_fwd",
    )(qb, kb, vb, dmat, win_f, wend_f, gch_f, bm)
    return pl.pallas_call(
        _ret_bwd_kernel,
        grid=(batch, nch + 1),
        in_specs=[spec(RET_QK_WIDTH, rmap), spec(RET_QK_WIDTH, rmap), spec(RET_V_WIDTH, rmap),
                  spec(RET_V_WIDTH, rmap), spec(RET_V_WIDTH, rmap),
                  full(win_b), full(wend_b), full(gch_b), full(bm), full(gn)],
        out_specs=spec(RET_V_WIDTH, spare(rmap)),
        out_shape=jax.ShapeDtypeStruct((out_rows, RET_V_WIDTH), MXU_DTYPE),
        scratch_shapes=[state],
        compiler_params=_cparams("arbitrary", "arbitrary"),
        name="retention_bwd",
    )(qb, kb, vb, o_f, gb, win_b, wend_b, gch_b, bm, gn)


def _s5_discretize(a_re, a_im, log_dt, b_re, b_im):
    a_re = jnp.minimum(a_re.astype(F32), -1e-4)
    a_im = a_im.astype(F32)
    dt = jnp.exp(log_dt.astype(F32))[..., None]
    mag = jnp.exp(dt * a_re)
    ab_re = mag * jnp.cos(dt * a_im)
    ab_im = mag * jnp.sin(dt * a_im)
    den = a_re * a_re + a_im * a_im
    nr = ab_re - 1.0
    f_re = ((nr * a_re + ab_im * a_im) / den)[..., None]
    f_im = ((ab_im * a_re - nr * a_im) / den)[..., None]
    br = b_re.astype(F32)[None]
    bi = b_im.astype(F32)[None]
    return ab_re, ab_im, f_re * br - f_im * bi, f_re * bi + f_im * br


def _complex_powers(ar, ai, n):
    pr = jnp.ones((1,) + ar.shape, F32)
    pi = jnp.zeros((1,) + ar.shape, F32)
    cr, ci = ar, ai
    while pr.shape[0] < n:
        pr, pi = (jnp.concatenate([pr, pr * cr - pi * ci], axis=0),
                  jnp.concatenate([pi, pr * ci + pi * cr], axis=0))
        cr, ci = cr * cr - ci * ci, 2.0 * cr * ci
    return pr[:n], pi[:n]


def _s5_operators(a_re, a_im, log_dt, b_re, b_im, c_re, c_im, levels):
    G, P, Cg, L = S5_GROUPS, S5_STATE, S5_GROUP_CH, S5_CHUNK
    ab_re, ab_im, bb_re, bb_im = _s5_discretize(a_re, a_im, log_dt, b_re, b_im)
    pr, pi = _complex_powers(ab_re, ab_im, L + 1)
    cr = c_re.astype(F32)
    ci = c_im.astype(F32)
    bt = jnp.concatenate([jnp.swapaxes(bb_re, 2, 3), jnp.swapaxes(bb_im, 2, 3)], axis=3)

    def powers(n, lo, descending):
        sel = lambda z: jnp.flip(z[lo:lo + L, n], axis=0) if descending else z[lo:lo + L, n]
        return sel(pr), sel(pi)

    def state_in(n, lo, descending):
        qr, qi = powers(n, lo, descending)
        wr = jnp.einsum('sgp,gpc->gscp', qr, bb_re[n]) - jnp.einsum('sgp,gpc->gscp', qi, bb_im[n])
        wi = jnp.einsum('sgp,gpc->gscp', qr, bb_im[n]) + jnp.einsum('sgp,gpc->gscp', qi, bb_re[n])
        return jnp.concatenate([wr, wi], axis=-1).reshape(G, L * Cg, 2 * P)

    def state_out(n, lo, descending):
        qr, qi = powers(n, lo, descending)
        er = jnp.einsum('tgp,gcp->gptc', qr, cr[n]) - jnp.einsum('tgp,gcp->gptc', qi, ci[n])
        ei = jnp.einsum('tgp,gcp->gptc', qr, ci[n]) + jnp.einsum('tgp,gcp->gptc', qi, cr[n])
        return jnp.concatenate([er, -ei], axis=1).reshape(G, 2 * P, L * Cg)

    ca = jnp.stack([state_out(0, 0, False), state_out(1, 0, True)])
    f = jnp.concatenate([state_in(0, 0, True), state_in(1, 0, False)], axis=-1).astype(MXU_DTYPE)
    e = jnp.stack([state_out(0, 1, False), state_out(1, 1, True)]).astype(MXU_DTYPE)
    return bt, ca, f, e, _s5_scan_mults(pr[L], pi[L], levels)


def _s5_scan_mults(alr, ali, levels):
    rows = []
    cr, ci = alr, ali
    for _ in range(levels):
        rows.append(jnp.stack([jnp.concatenate([cr, cr], -1), jnp.concatenate([-ci, ci], -1)], axis=-2))
        cr, ci = cr * cr - ci * ci, 2.0 * cr * ci
    return jnp.stack(rows, axis=2)


def _shift_rows(x, k, down):
    n = x.shape[0]
    row = lax.broadcasted_iota(jnp.int32, (n, 1), 0)
    if down:
        return jnp.where(row >= k, pltpu.roll(x, k, 0), 0.0)
    return jnp.where(row < n - k, pltpu.roll(x, n - k, 0), 0.0)


def _swap_halves(x):
    return pltpu.roll(x, S5_STATE, 1)


def _s5_scans(items):
    n = items[0][0].shape[0]
    row = lax.broadcasted_iota(jnp.int32, (n, 1), 0)
    first = [(row == (0 if fwd else n - 1)).astype(F32) for *_, fwd in items]
    times = lambda m_ref, lvl, z: m_ref[lvl, 0:1] * z + m_ref[lvl, 1:2] * _swap_halves(z)
    xs = [loc + f * times(m_ref, 0, init) for (loc, m_ref, init, _), f in zip(items, first)]
    k, lvl = 1, 0
    while k < n:
        shifted = [_shift_rows(x, k, it[3]) for x, it in zip(xs, items)]
        xs = [x + times(it[1], lvl, sx) for x, sx, it in zip(xs, shifted, items)]
        k, lvl = 2 * k, lvl + 1
    out = []
    for x, it, f in zip(xs, items, first):
        last = slice(n - 1, n) if it[3] else slice(0, 1)
        out.append((_shift_rows(x, 1, it[3]) + f * it[2], x[last]))
    return out


def _s5_toeplitz(bt_ref, ca_ref, conv_ref):
    L, Cg = S5_CHUNK, S5_GROUP_CH
    width = L * Cg
    hi = lax.Precision.HIGHEST
    kf = jnp.dot(bt_ref[0], ca_ref[0], preferred_element_type=F32, precision=hi)
    kb = jnp.dot(bt_ref[1], ca_ref[1], preferred_element_type=F32, precision=hi)
    lane = lax.broadcasted_iota(jnp.int32, (Cg, width), 1)
    for s in range(L):
        right = s * Cg
        left = (L - 1 - s) * Cg
        fwd = kf if right == 0 else jnp.where(lane >= right, pltpu.roll(kf, right, 1), 0.0)
        bwd = kb if left == 0 else jnp.where(lane < width - left, pltpu.roll(kb, width - left, 1), 0.0)
        conv_ref[s * Cg:(s + 1) * Cg, :] = fwd + bwd


def _s5_kernel(u_ref, bt_ref, ca_ref, fcat_ref, e_ref, mult_ref, d_ref, y_ref, conv_ref, *, batch, n_lat_ch, n_ctx_ch):
    P2 = 2 * S5_STATE
    u = u_ref[...]
    ub = u.astype(MXU_DTYPE)
    loc = _mxu(ub, fcat_ref[...])
    _s5_toeplitz(bt_ref, ca_ref, conv_ref)
    y_ref[...] = _mxu(ub, conv_ref[...]) + u * d_ref[...]
    zero = jnp.zeros((1, P2), F32)
    chains = [(b, n) for b in range(batch) for n in range(2)]
    ctx_rows = lambda b: slice(batch * n_lat_ch + b * n_ctx_ch, batch * n_lat_ch + (b + 1) * n_ctx_ch)
    lat_rows = lambda b: slice(b * n_lat_ch, (b + 1) * n_lat_ch)
    item = lambda rows, n, init: (loc[rows, n * P2:(n + 1) * P2], mult_ref.at[n], init, n == 0)
    ctx_out = _s5_scans([item(ctx_rows(b), n, zero) for b, n in chains])
    lat_out = _s5_scans([item(lat_rows(b), n, x_last) for (b, n), (_, x_last) in zip(chains, ctx_out)])
    for (b, n), (prev_c, _), (prev_l, _) in zip(chains, ctx_out, lat_out):
        y_ref[ctx_rows(b), :] += _mxu(prev_c, e_ref[n])
        y_ref[lat_rows(b), :] += _mxu(prev_l, e_ref[n])


def _s5_mixer(ug, ops, d_skip, layer, batch, n_lat, n_ctx):
    bt, ca, f_cat, e, mults = ops
    G, Cg, L, P2 = S5_GROUPS, S5_GROUP_CH, S5_CHUNK, 2 * S5_STATE
    rows, width = ug.shape[1], L * Cg
    levels = mults.shape[3]
    per_g = lambda *s: pl.BlockSpec((None,) + s, lambda g: (g,) + (0,) * len(s))
    per_dir = lambda *s: pl.BlockSpec((None, 2, None) + s, lambda g: (layer, 0, g) + (0,) * len(s))
    return pl.pallas_call(
        functools.partial(_s5_kernel, batch=batch, n_lat_ch=n_lat // L, n_ctx_ch=n_ctx // L),
        grid=(G,),
        in_specs=[per_g(rows, width), per_dir(Cg, P2), per_dir(P2, width),
                  pl.BlockSpec((None, None, width, 2 * P2), lambda g: (layer, g, 0, 0)),
                  per_dir(P2, width), per_dir(levels, 2, P2),
                  pl.BlockSpec((None, None, 1, width), lambda g: (layer, g, 0, 0))],
        out_specs=per_g(rows, width),
        out_shape=jax.ShapeDtypeStruct((G, rows, width), F32),
        scratch_shapes=[pltpu.VMEM((width, width), F32)],
        compiler_params=_cparams("arbitrary"),
        name="s5_conv",
    )(ug, bt, ca, f_cat, e, mults, d_skip)


def _mix_ffn_kernel(x_ref, mod_ref, ya_ref, yb_ref, yg_ref, wglu_ref, wo_ref, win_ref, wout_ref, o_ref, tok_ref,
                    *, tiles_per_batch):
    d = x_ref.shape[1]
    row = pl.program_id(0) // tiles_per_batch
    mix_ab = (_mxu(ya_ref[...], wo_ref[0:NA_WIDTH, :])
              + _mxu(yb_ref[...], wo_ref[NA_WIDTH:NA_WIDTH + RET_V_WIDTH, :]))
    _groups_to_tokens(yg_ref, tok_ref)
    yc = jnp.concatenate([tok_ref[hb] for hb in range(S5_WIDTH // LANES)], axis=1)
    glu = _mxu(jax.nn.gelu(yc), wglu_ref[...])
    s5 = glu[:, :S5_WIDTH] * jax.nn.sigmoid(glu[:, S5_WIDTH:])
    mix = mix_ab + _mxu(s5, wo_ref[NA_WIDTH + RET_V_WIDTH:, :])
    x = x_ref[...] + _mod_rows(mod_ref, row, 5, d) * mix
    o_ref[...] = _ffn_body(x, mod_ref, row, 6, win_ref, wout_ref)


def _mix_ffn(x_all, mods_l, ya, yb, yg, w_glu, w_out, w_in2, w_out2, layer, tiles_per_batch, n_tiles):
    d = x_all.shape[1]
    tm = TOKEN_TILE
    tok = lambda w: pl.BlockSpec((tm, w), lambda i: (i, 0))
    return pl.pallas_call(
        functools.partial(_mix_ffn_kernel, tiles_per_batch=tiles_per_batch),
        grid=(n_tiles,),
        in_specs=[tok(d), pl.BlockSpec(mods_l.shape, lambda i: (0, 0)), tok(NA_WIDTH), tok(RET_V_WIDTH),
                  pl.BlockSpec((S5_GROUPS, tm // S5_CHUNK, yg.shape[2]), lambda i: (0, i, 0)),
                  _resident(w_glu.shape, layer), _resident(w_out.shape, layer),
                  _resident(w_in2.shape, layer), _resident(w_out2.shape, layer)],
        out_specs=tok(d),
        out_shape=jax.ShapeDtypeStruct((n_tiles * tm, d), F32),
        scratch_shapes=[_token_scratch(tm)],
        compiler_params=_cparams("arbitrary"),
        name="mix_ffn2",
    )(x_all, mods_l, ya, yb, yg, w_glu, w_out, w_in2, w_out2)


def _rope_tables(n_lat, n_ctx):
    nf = RET_QK_DIM // 4
    inv = ROPE_BASE ** (-jnp.arange(nf, dtype=F32) / nf)
    t = jnp.arange(n_lat)
    row = (t // GRID_W).astype(F32)
    col = (t % GRID_W).astype(F32)
    ang = jnp.concatenate([row[:, None] * inv, col[:, None] * inv], axis=-1)
    cos, sin = jnp.cos(ang), jnp.sin(ang)
    cos_l = jnp.tile(jnp.concatenate([cos, cos], axis=-1), (1, RET_HEADS))
    sin_l = jnp.tile(jnp.concatenate([-sin, sin], axis=-1), (1, RET_HEADS))
    pad = TOKEN_TILE
    return (jnp.concatenate([cos_l, jnp.ones((pad, RET_QK_WIDTH), F32)], axis=0),
            jnp.concatenate([sin_l, jnp.zeros((pad, RET_QK_WIDTH), F32)], axis=0))


def kernel(x, c, ctx, c_ctx, w_mod, b_mod, ffn1_w_in, ffn1_w_out, w_in, w_out, na_q_gain, na_k_gain, na_rpb,
           ret_decay, ret_gn, s5_a_re, s5_a_im, s5_log_dt, s5_b_re, s5_b_im, s5_c_re, s5_c_im, s5_d, s5_w_glu,
           ffn2_w_in, ffn2_w_out):
    batch, n_lat, d = x.shape
    n_ctx = ctx.shape[1]
    depth = w_mod.shape[0]
    assert n_ctx == NA_BLOCK == RET_CHUNK and batch * n_ctx == TOKEN_TILE
    assert n_lat % TOKEN_TILE == 0 and n_lat // NA_BLOCK >= 4
    assert batch + 1 <= 8
    tiles_per_batch = n_lat // TOKEN_TILE
    n_lat_tiles = batch * tiles_per_batch
    rows = n_lat // GRID_W

    bf = lambda a: a.astype(MXU_DTYPE)
    ffn1_w_in, ffn1_w_out, ffn2_w_in, ffn2_w_out = bf(ffn1_w_in), bf(ffn1_w_out), bf(ffn2_w_in), bf(ffn2_w_out)
    w_in, w_out, s5_w_glu = bf(w_in), bf(w_out), bf(s5_w_glu)

    cond = jnp.concatenate([c, c_ctx[None, :], jnp.zeros((8 - batch - 1, d), F32)], axis=0)
    mods = _modulation(cond, w_mod, b_mod)

    head_sum = jnp.asarray(np.arange(NA_WIDTH)[:, None] // NA_HEAD_DIM == np.arange(NA_WIDTH)[None, :] // NA_HEAD_DIM,
                           MXU_DTYPE)
    cos_tab, sin_tab = _rope_tables(n_lat, n_ctx)
    levels = max(1, int(math.ceil(math.log2(max(n_lat, n_ctx) // S5_CHUNK))))
    s5_ops = jax.vmap(functools.partial(_s5_operators, levels=levels))(
        s5_a_re, s5_a_im, s5_log_dt, s5_b_re, s5_b_im, s5_c_re, s5_c_im)

    na_bias = _na_bias_tables(na_rpb, rows)
    ret_tables = jax.vmap(_ret_tables)(ret_decay)
    ret_gain = ret_gn.astype(F32).reshape(depth, 1, RET_V_WIDTH)
    q_gain = jnp.tile(na_q_gain.astype(F32), (1, NA_HEADS)).reshape(depth, 1, NA_WIDTH)
    k_gain = jnp.tile(na_k_gain.astype(F32), (1, NA_HEADS)).reshape(depth, 1, NA_WIDTH)
    s5_skip = jnp.tile(s5_d.astype(F32).reshape(depth, S5_GROUPS, 1, S5_GROUP_CH), (1, 1, 1, S5_CHUNK))

    x_all = x.reshape(batch * n_lat, d)
    x_ctx = ctx.reshape(batch * n_ctx, d)
    for l in range(depth):
        mods_l = mods[l]
        x_all = _ffn_first(x_all, x_ctx if l == 0 else None, mods_l, ffn1_w_in, ffn1_w_out, l, tiles_per_batch)
        qa, ka, va, qb, kb, vb, gb, ug = _inproj(x_all, mods_l, w_in, l, head_sum, q_gain[l], k_gain[l],
                                                 cos_tab, sin_tab, tiles_per_batch, n_lat_tiles)
        ya = _neighborhood_attention(qa, ka, va, na_bias, l, batch, n_lat)
        yb = _retention(qb, kb, vb, gb, ret_tables, ret_gain, l, batch, n_lat)
        yg = _s5_mixer(ug, s5_ops, s5_skip, l, batch, n_lat, n_ctx)
        n_tiles = n_lat_tiles + 1 if l < depth - 1 else n_lat_tiles
        x_all = _mix_ffn(x_all, mods_l, ya, yb, yg, s5_w_glu, w_out, ffn2_w_in, ffn2_w_out, l, tiles_per_batch, n_tiles)
    return x_all.reshape(batch, n_lat, d)
```

```python
import functools
import math

import numpy as np
import jax
import jax.numpy as jnp
from jax import lax
from jax.experimental import pallas as pl
from jax.experimental.pallas import tpu as pltpu

F32 = jnp.float32
MXU_DTYPE = jnp.bfloat16

GRID_W = 64
N_MOD = 9
NORM_EPS = 1e-6
NA_HEADS, NA_HEAD_DIM, NA_ROWS, NA_COLS = 8, 32, 8, 16
RET_HEADS, RET_QK_DIM, RET_V_DIM = 4, 64, 128
ROPE_BASE = 10000.0
S5_GROUPS, S5_GROUP_CH, S5_STATE = 16, 16, 64
NA_WIDTH = NA_HEADS * NA_HEAD_DIM
RET_QK_WIDTH = RET_HEADS * RET_QK_DIM
RET_V_WIDTH = RET_HEADS * RET_V_DIM
S5_WIDTH = S5_GROUPS * S5_GROUP_CH

TOKEN_TILE = 512
FFN_CHUNK = 256
NA_BLOCK_ROWS = 4
NA_BLOCK = NA_BLOCK_ROWS * GRID_W
RET_CHUNK = 256
S5_CHUNK = 32
MASK_VALUE = -1e30
LOG2_E = math.log2(math.e)
LANES = 128
VMEM_LIMIT = 56 * 1024 * 1024


def _mxu(a, b):
    return jnp.dot(a.astype(MXU_DTYPE), b.astype(MXU_DTYPE), preferred_element_type=F32)


def _mxu_nt(a, b):
    return lax.dot_general(a.astype(MXU_DTYPE), b.astype(MXU_DTYPE), (((1,), (1,)), ((), ())),
                           preferred_element_type=F32)


def _mxu_tn(a, b):
    return lax.dot_general(a.astype(MXU_DTYPE), b.astype(MXU_DTYPE), (((0,), (0,)), ((), ())),
                           preferred_element_type=F32)


def _silu(x):
    return x * jax.nn.sigmoid(x)


def _cparams(*sem):
    return pltpu.CompilerParams(dimension_semantics=sem, vmem_limit_bytes=VMEM_LIMIT)


def _mod_kernel(s_ref, w_ref, b_ref, o_ref):
    s = _silu(s_ref[...])
    o_ref[...] = _mxu(s, w_ref[...]) + b_ref[...]


def _modulation(cond_rows, w_mod, b_mod):
    depth, d, nd = w_mod.shape
    tn = 1024
    return pl.pallas_call(
        _mod_kernel,
        grid=(depth, nd // tn),
        in_specs=[pl.BlockSpec((8, d), lambda l, j: (0, 0)),
                  pl.BlockSpec((None, d, tn), lambda l, j: (l, 0, j)),
                  pl.BlockSpec((None, 1, tn), lambda l, j: (l, 0, j))],
        out_specs=pl.BlockSpec((None, 8, tn), lambda l, j: (l, 0, j)),
        out_shape=jax.ShapeDtypeStruct((depth, 8, nd), F32),
        compiler_params=_cparams("arbitrary", "arbitrary"),
        name="modulation",
    )(cond_rows, w_mod, b_mod.reshape(depth, 1, nd))


def _mod_rows(mod_ref, row, j, d):
    return mod_ref[pl.ds(row, 1), j * d:(j + 1) * d]


def _norm_mod(x, shift, scale):
    ms = jnp.mean(x * x, axis=-1, keepdims=True)
    return (x * lax.rsqrt(ms + NORM_EPS)) * (1.0 + scale) + shift


def _ffn_body(x, mod_ref, row, j0, win_ref, wout_ref):
    d = x.shape[1]
    dff = wout_ref.shape[0]
    h = _norm_mod(x, _mod_rows(mod_ref, row, j0, d), _mod_rows(mod_ref, row, j0 + 1, d)).astype(MXU_DTYPE)
    acc = jnp.zeros(x.shape, F32)
    for c in range(dff // FFN_CHUNK):
        lo = c * FFN_CHUNK
        a = _mxu(h, win_ref[:, lo:lo + FFN_CHUNK])
        b = _mxu(h, win_ref[:, dff + lo:dff + lo + FFN_CHUNK])
        acc = acc + _mxu(_silu(a) * b, wout_ref[lo:lo + FFN_CHUNK, :])
    return x + (0.5 * _mod_rows(mod_ref, row, j0 + 2, d)) * acc


def _ffn_kernel(x_ref, mod_ref, win_ref, wout_ref, o_ref, *, tiles_per_batch):
    row = pl.program_id(0) // tiles_per_batch
    o_ref[...] = _ffn_body(x_ref[...], mod_ref, row, 0, win_ref, wout_ref)


def _ffn_split_kernel(x_ref, ctx_ref, mod_ref, win_ref, wout_ref, o_ref, *, tiles_per_batch, n_lat_tiles):
    i = pl.program_id(0)
    x = jnp.where(i < n_lat_tiles, x_ref[...], ctx_ref[...])
    o_ref[...] = _ffn_body(x, mod_ref, i // tiles_per_batch, 0, win_ref, wout_ref)


def _resident(shape, layer):
    return pl.BlockSpec((None,) + tuple(shape[1:]), lambda i: (layer,) + (0,) * (len(shape) - 1),
                        pipeline_mode=pl.Buffered(1))


def _ffn_first(x_lat, x_ctx, mods_l, w_in, w_out, layer, tiles_per_batch):
    d = x_lat.shape[1]
    tm = TOKEN_TILE
    tok = pl.BlockSpec((tm, d), lambda i: (i, 0))
    mod = pl.BlockSpec(mods_l.shape, lambda i: (0, 0))
    weights = [_resident(w_in.shape, layer), _resident(w_out.shape, layer)]
    if x_ctx is None:
        t_all = x_lat.shape[0]
        kern = functools.partial(_ffn_kernel, tiles_per_batch=tiles_per_batch)
        in_specs, args = [tok, mod] + weights, (x_lat, mods_l, w_in, w_out)
    else:
        n_lat_tiles = x_lat.shape[0] // tm
        t_all = x_lat.shape[0] + x_ctx.shape[0]
        kern = functools.partial(_ffn_split_kernel, tiles_per_batch=tiles_per_batch, n_lat_tiles=n_lat_tiles)
        in_specs = [pl.BlockSpec((tm, d), lambda i: (jnp.minimum(i, n_lat_tiles - 1), 0)),
                    pl.BlockSpec((tm, d), lambda i: (0, 0)), mod] + weights
        args = (x_lat, x_ctx, mods_l, w_in, w_out)
    return pl.pallas_call(
        kern, grid=(t_all // tm,), in_specs=in_specs, out_specs=tok,
        out_shape=jax.ShapeDtypeStruct((t_all, d), F32),
        compiler_params=_cparams("arbitrary"), name="ffn1",
    )(*args)


def _transpose_blocks(v):
    nb = len(v)
    lane_blk = lax.broadcasted_iota(jnp.int32, v[0].shape, 1) // S5_GROUP_CH
    s = nb // 2
    while s >= 1:
        upper = (lane_blk & s) != 0
        out = list(v)
        for a in range(nb):
            if a & s:
                continue
            b = a + s
            out[a] = jnp.where(upper, pltpu.roll(v[b], s * S5_GROUP_CH, 1), v[a])
            out[b] = jnp.where(upper, v[b], pltpu.roll(v[a], LANES - s * S5_GROUP_CH, 1))
        v = out
        s //= 2
    return v


def _token_scratch(tm):
    return pltpu.VMEM((S5_WIDTH // LANES, tm, LANES), F32)


def _tokens_to_groups(tok_ref, ug_ref):
    L, nb = S5_CHUNK, LANES // S5_GROUP_CH
    n = tok_ref.shape[1] // L
    for hb in range(tok_ref.shape[0]):
        for q in range(L // nb):
            by_group = _transpose_blocks([tok_ref[hb, pl.ds(q * nb + r, n, stride=L), :] for r in range(nb)])
            for j in range(nb):
                ug_ref[hb * nb + j, :, q * LANES:(q + 1) * LANES] = by_group[j]


def _groups_to_tokens(yg_ref, tok_ref):
    L, nb = S5_CHUNK, LANES // S5_GROUP_CH
    n = tok_ref.shape[1] // L
    for hb in range(tok_ref.shape[0]):
        for q in range(L // nb):
            by_token = _transpose_blocks([yg_ref[hb * nb + j, :, q * LANES:(q + 1) * LANES] for j in range(nb)])
            for r in range(nb):
                tok_ref[hb, pl.ds(q * nb + r, n, stride=L), :] = by_token[r]


def _head_rms(z, head_sum, gain):
    zz = z * z
    hi = zz.astype(MXU_DTYPE)
    lo = zz - hi.astype(F32)
    ms = (_mxu(hi, head_sum) + _mxu(lo, head_sum)) * (1.0 / NA_HEAD_DIM)
    return z * lax.rsqrt(ms + NORM_EPS) * gain


def _rotate(z, cos, sin_signed):
    tm = z.shape[0]
    lane = lax.broadcasted_iota(jnp.int32, (tm, 128), 1)
    first_half = (lane % RET_QK_DIM) < (RET_QK_DIM // 2)
    outs = []
    for c in range(z.shape[1] // 128):
        zc = z[:, c * 128:(c + 1) * 128]
        partner = jnp.where(first_half, pltpu.roll(zc, 128 - RET_QK_DIM // 2, 1), pltpu.roll(zc, RET_QK_DIM // 2, 1))
        outs.append(zc * cos[:, c * 128:(c + 1) * 128] + partner * sin_signed[:, c * 128:(c + 1) * 128])
    return jnp.concatenate(outs, axis=1)


def _inproj_kernel(x_ref, mod_ref, w_ref, hs_ref, qg_ref, kg_ref, cos_ref, sin_ref,
                   qa_ref, ka_ref, va_ref, qb_ref, kb_ref, vb_ref, gb_ref, ug_ref, tok_ref, *, tiles_per_batch):
    d = x_ref.shape[1]
    row = pl.program_id(0) // tiles_per_batch
    x = x_ref[...]
    h = _norm_mod(x, _mod_rows(mod_ref, row, 3, d), _mod_rows(mod_ref, row, 4, d)).astype(MXU_DTYPE)

    def proj(lo, width):
        return _mxu(h, w_ref[:, lo:lo + width])

    widths = (NA_WIDTH, NA_WIDTH, NA_WIDTH, RET_QK_WIDTH, RET_QK_WIDTH, RET_V_WIDTH, RET_V_WIDTH, S5_WIDTH)
    off = [sum(widths[:i]) for i in range(len(widths))]
    z_u, z_qa, z_ka, z_qb, z_kb = (proj(off[i], widths[i]) for i in (7, 0, 1, 3, 4))
    z_va, z_vb, z_gb = (proj(off[i], widths[i]) for i in (2, 5, 6))
    for hb in range(S5_WIDTH // LANES):
        tok_ref[hb] = z_u[:, hb * LANES:(hb + 1) * LANES]
    _tokens_to_groups(tok_ref, ug_ref)
    head_sum = hs_ref[...]
    qa_ref[...] = (_head_rms(z_qa, head_sum, qg_ref[...]) * (NA_HEAD_DIM ** -0.5 * LOG2_E)).astype(qa_ref.dtype)
    ka_ref[...] = _head_rms(z_ka, head_sum, kg_ref[...]).astype(ka_ref.dtype)
    cos = cos_ref[...]
    sin = sin_ref[...]
    qb_ref[...] = _rotate(z_qb, cos, sin).astype(qb_ref.dtype)
    kb_ref[...] = (_rotate(z_kb, cos, sin) * (RET_QK_DIM ** -0.5)).astype(kb_ref.dtype)
    va_ref[...] = z_va.astype(va_ref.dtype)
    vb_ref[...] = z_vb.astype(vb_ref.dtype)
    gb_ref[...] = z_gb


def _inproj(x_all, mods_l, w_in, layer, head_sum, q_gain, k_gain, cos_tab, sin_tab, tiles_per_batch, n_lat_tiles):
    t_all, d = x_all.shape
    tm = TOKEN_TILE
    tok = lambda w: pl.BlockSpec((tm, w), lambda i: (i, 0))
    full = lambda a: pl.BlockSpec(a.shape, lambda i: (0,) * a.ndim)
    tab = pl.BlockSpec((tm, RET_QK_WIDTH),
                       lambda i: (jnp.where(i < n_lat_tiles, i % tiles_per_batch, tiles_per_batch), 0))
    chunk_w = S5_CHUNK * S5_GROUP_CH
    token_major = lambda w, dt: (tok(w), jax.ShapeDtypeStruct((t_all, w), dt))
    outs = [token_major(NA_WIDTH, MXU_DTYPE), token_major(NA_WIDTH, MXU_DTYPE), token_major(NA_WIDTH, MXU_DTYPE),
            token_major(RET_QK_WIDTH, MXU_DTYPE), token_major(RET_QK_WIDTH, MXU_DTYPE),
            token_major(RET_V_WIDTH, MXU_DTYPE), token_major(RET_V_WIDTH, F32),
            (pl.BlockSpec((S5_GROUPS, tm // S5_CHUNK, chunk_w), lambda i: (0, i, 0)),
             jax.ShapeDtypeStruct((S5_GROUPS, t_all // S5_CHUNK, chunk_w), F32))]
    return pl.pallas_call(
        functools.partial(_inproj_kernel, tiles_per_batch=tiles_per_batch),
        grid=(t_all // tm,),
        in_specs=[tok(d), full(mods_l), _resident(w_in.shape, layer),
                  full(head_sum), full(q_gain), full(k_gain), tab, tab],
        out_specs=[o[0] for o in outs],
        out_shape=[o[1] for o in outs],
        scratch_shapes=[_token_scratch(tm)],
        compiler_params=_cparams("arbitrary"),
        name="inproj",
    )(x_all, mods_l, w_in, head_sum, q_gain, k_gain, cos_tab, sin_tab)


def _softmax_heads(q, parts):
    lane = lax.broadcasted_iota(jnp.int32, (1, NA_WIDTH), 1)
    acc = jnp.zeros((q.shape[0], NA_WIDTH), F32)
    ones_lane = lambda h: 0 if h != 0 else NA_HEAD_DIM
    v_ones = {ln: [jnp.where(lane == ln, jnp.ones_like(v), v) for _, v, _ in parts] for ln in {ones_lane(h) for h in range(NA_HEADS)}}
    heads = range(NA_HEADS)
    in_head = [(lane // NA_HEAD_DIM) == h for h in heads]
    scores = []
    for h in heads:
        qh = jnp.where(in_head[h], q, jnp.zeros_like(q))
        scores.append([_mxu_nt(qh, k) if bias_fn is None else _mxu_nt(qh, k) + bias_fn(h) for k, _, bias_fn in parts])
    maxes = [functools.reduce(jnp.maximum, [jnp.max(s, axis=-1, keepdims=True) for s in scores[h]]) for h in heads]
    probs = [[jnp.exp2((s - maxes[h]).astype(MXU_DTYPE)) for s in scores[h]] for h in heads]
    outs = [functools.reduce(jnp.add, [_mxu(p, v) for p, v in zip(probs[h], v_ones[ones_lane(h)])]) for h in heads]
    for h in heads:
        ln = ones_lane(h)
        acc = acc + outs[h] * jnp.where(in_head[h], 1.0 / outs[h][:, ln:ln + 1], 0.0)
    return acc


def _na_kernel(q_ref, k0_ref, k1_ref, k2_ref, v0_ref, v1_ref, v2_ref, kc_ref, vc_ref, bias_ref, win_ref, o_ref,
               table_ref, *, nblk):
    i = pl.program_id(1)

    @pl.when((i <= 1) | (i == nblk - 1))
    def _():
        for h in range(NA_HEADS):
            table_ref[h] = bias_ref[h] + win_ref[...]

    q = q_ref[...]
    kc = kc_ref[...]
    vc = vc_ref[...]

    @pl.when(i < nblk)
    def _():
        k = jnp.concatenate([k0_ref[...], k1_ref[...], k2_ref[...]], axis=0)
        v = jnp.concatenate([v0_ref[...], v1_ref[...], v2_ref[...]], axis=0)
        out = _softmax_heads(q, [(k, v, lambda h: table_ref[h]), (kc, vc, None)])
        o_ref[...] = out.astype(o_ref.dtype)

    @pl.when(i == nblk)
    def _():
        o_ref[...] = _softmax_heads(q, [(kc, vc, None)]).astype(o_ref.dtype)


def _na_bias_tables(rpb, rows):
    depth = rpb.shape[0]
    nc = 2 * NA_COLS - 1
    lead = GRID_W - NA_COLS
    padded = jnp.pad(rpb.astype(F32) * LOG2_E, ((0, 0),) * 3 + ((lead, 2 * GRID_W - lead - nc),))
    flat = jnp.tile(padded, (1, 1, 1, GRID_W))[..., :GRID_W * (2 * GRID_W - 1)]
    col = flat.reshape(rpb.shape[:3] + (GRID_W, 2 * GRID_W - 1))[..., GRID_W - 1:]
    n_key_rows = 3 * NA_BLOCK_ROWS
    per_a = [col[:, :, NA_ROWS - 1 - NA_BLOCK_ROWS - a:][:, :, :n_key_rows] for a in range(NA_BLOCK_ROWS)]
    base = jnp.transpose(jnp.stack(per_a, axis=2), (0, 1, 2, 4, 3, 5))
    base = base.reshape(depth, NA_HEADS, NA_BLOCK, n_key_rows * GRID_W)
    w = np.arange(GRID_W)
    kc = np.arange(GRID_W)
    valid = np.zeros((3, NA_BLOCK, n_key_rows * GRID_W), bool)
    cs = np.clip(w - NA_COLS // 2, 0, GRID_W - NA_COLS)
    col_ok = (kc[None, :] >= cs[:, None]) & (kc[None, :] < cs[:, None] + NA_COLS)
    for var, r0 in enumerate((0, 2 * NA_BLOCK_ROWS, rows - NA_BLOCK_ROWS)):
        for a in range(NA_BLOCK_ROWS):
            r = r0 + a
            rs = np.clip(r - NA_ROWS // 2, 0, rows - NA_ROWS)
            kr = r0 - NA_BLOCK_ROWS + np.arange(n_key_rows)
            row_ok = (kr >= 0) & (kr < rows) & (kr >= rs) & (kr < rs + NA_ROWS)
            ok = row_ok[None, :, None] & col_ok[:, None, :]
            valid[var, a * GRID_W:(a + 1) * GRID_W] = ok.reshape(GRID_W, -1)
    return base, jnp.asarray(np.where(valid, 0.0, MASK_VALUE), F32)


def _neighborhood_attention(qa, ka, va, bias, window, layer, batch, n_lat):
    t_all = qa.shape[0]
    nblk = n_lat // NA_BLOCK
    ctx0 = batch * nblk
    blk = (NA_BLOCK, NA_WIDTH)

    def qmap(b, i):
        return (jnp.where(i < nblk, b * nblk + i, ctx0 + b), 0)

    kspec = lambda off: pl.BlockSpec(blk, lambda b, i: (b * nblk + jnp.clip(i + off, 0, nblk - 1), 0))
    cspec = pl.BlockSpec(blk, lambda b, i: (ctx0 + b, 0))
    wmap = lambda b, i: (jnp.where(i == 0, 0, jnp.where(i >= nblk - 1, 2, 1)), 0, 0)
    return pl.pallas_call(
        functools.partial(_na_kernel, nblk=nblk),
        grid=(batch, nblk + 1),
        in_specs=[pl.BlockSpec(blk, qmap),
                  kspec(-1), kspec(0), kspec(1), kspec(-1), kspec(0), kspec(1), cspec, cspec,
                  pl.BlockSpec((None,) + bias.shape[1:], lambda b, i: (layer, 0, 0, 0)),
                  pl.BlockSpec((None,) + window.shape[1:], wmap)],
        out_specs=pl.BlockSpec(blk, qmap),
        out_shape=jax.ShapeDtypeStruct((t_all, NA_WIDTH), MXU_DTYPE),
        scratch_shapes=[pltpu.VMEM(bias.shape[1:], F32)],
        compiler_params=_cparams("arbitrary", "arbitrary"),
        name="natten",
    )(qa, ka, ka, ka, va, va, va, ka, va, bias, window)


def _ret_next_state(state, k, v, w_end, g_chunk, block_mask):
    return g_chunk * state + _mxu_tn(k.astype(F32) * w_end, v) * block_mask


def _ret_sweep(chunk_fn, state_ref, order):
    b = pl.program_id(0)
    i = pl.program_id(1)

    @pl.when(i == 0)
    def _():
        zero = jnp.zeros(state_ref.shape, F32)
        finals = [chunk_fn(j, zero) for j in range(len(order))]
        carried = finals[0]
        for j in range(1, len(order)):
            carried = jnp.where(b == j, finals[j], carried)
        state_ref[...] = carried

    @pl.when(i > 0)
    def _():
        state = state_ref[...]
        for j in order:
            state = chunk_fn(j, state)
        state_ref[...] = state


def _ret_fwd_kernel(q_ref, k_ref, v_ref, dmat_ref, win_ref, wend_ref, gch_ref, bm_ref, o_ref, state_ref):
    c = RET_CHUNK
    lane = lax.broadcasted_iota(jnp.int32, (1, RET_QK_WIDTH), 1)
    heads = range(RET_HEADS)

    def chunk(j, state):
        rows = slice(j * c, (j + 1) * c)
        q, k, v = q_ref[rows, :], k_ref[rows, :], v_ref[rows, :]
        cross = _mxu(q, state) * win_ref[...]
        scores = [_mxu_nt(jnp.where((lane // RET_QK_DIM) == h, q, jnp.zeros_like(q)), k) for h in heads]
        decayed = [scores[h] * dmat_ref[h] for h in heads]
        for h in heads:
            sl = slice(h * RET_V_DIM, (h + 1) * RET_V_DIM)
            o_ref[rows, sl] = _mxu(decayed[h], v[:, sl]) + cross[:, sl]
        return _ret_next_state(state, k, v, wend_ref[...], gch_ref[...], bm_ref[...])

    _ret_sweep(chunk, state_ref, tuple(range(q_ref.shape[0] // c)))


def _ret_bwd_kernel(q_ref, k_ref, v_ref, of_ref, gate_ref, win_ref, wend_ref, gch_ref, bm_ref, gn_ref,
                    y_ref, state_ref):
    c = RET_CHUNK
    heads = range(RET_HEADS)
    sls = [slice(h * RET_V_DIM, (h + 1) * RET_V_DIM) for h in heads]

    def chunk(j, state):
        rows = slice(j * c, (j + 1) * c)
        o = of_ref[rows, :] + _mxu(q_ref[rows, :], state) * win_ref[...]
        gate = _silu(gate_ref[rows, :])
        gain = gn_ref[...]
        cen = [o[:, sl] - jnp.mean(o[:, sl], axis=-1, keepdims=True) for sl in sls]
        inv = [lax.rsqrt(jnp.mean(x * x, axis=-1, keepdims=True) + NORM_EPS) for x in cen]
        for h in heads:
            y_ref[rows, sls[h]] = (gate[:, sls[h]] * (cen[h] * inv[h] * gain[:, sls[h]])).astype(y_ref.dtype)
        return _ret_next_state(state, k_ref[rows, :], v_ref[rows, :], wend_ref[...], gch_ref[...], bm_ref[...])

    _ret_sweep(chunk, state_ref, tuple(reversed(range(q_ref.shape[0] // c))))


def _ret_tables(decay):
    c = RET_CHUNK
    log_g = jax.nn.log_sigmoid(decay.astype(F32))
    pos = jnp.arange(c, dtype=F32)
    diff = pos[:, None] - pos[None, :]
    lgf = log_g[0][:, None, None]
    lgb = log_g[1][:, None, None]
    dmat = jnp.where(diff >= 0, jnp.exp(lgf * jnp.maximum(diff, 0.0)), jnp.exp(lgb * jnp.maximum(-diff, 0.0)))
    rep = lambda a, w: jnp.repeat(a, w, axis=-1)
    win_f = rep(jnp.exp(log_g[0][None, :] * (pos[:, None] + 1.0)), RET_V_DIM)
    wend_f = rep(jnp.exp(log_g[0][None, :] * (c - 1.0 - pos[:, None])), RET_QK_DIM)
    win_b = rep(jnp.exp(log_g[1][None, :] * (c - pos[:, None])), RET_V_DIM)
    wend_b = rep(jnp.exp(log_g[1][None, :] * pos[:, None]), RET_QK_DIM)
    gch_f = rep(jnp.exp(log_g[0] * c)[None, :], RET_V_DIM)
    gch_b = rep(jnp.exp(log_g[1] * c)[None, :], RET_V_DIM)
    return dmat, (win_f, wend_f, gch_f), (win_b, wend_b, gch_b)


def _retention(qb, kb, vb, gb, tables, gn, layer, batch, n_lat):
    t_all = qb.shape[0]
    c = batch * RET_CHUNK
    nch = n_lat // c
    ctx0 = batch * nch
    dmat, (win_f, wend_f, gch_f), (win_b, wend_b, gch_b) = tables
    bm = (np.arange(RET_QK_WIDTH)[:, None] // RET_QK_DIM == np.arange(RET_V_WIDTH)[None, :] // RET_V_DIM)
    bm = jnp.asarray(bm, F32)
    fmap = lambda b, i: (jnp.where(i == 0, ctx0, b * nch + i - 1), 0)
    rmap = lambda b, i: (jnp.where(i == 0, ctx0, b * nch + nch - i), 0)
    spare = lambda m: (lambda b, i: (jnp.where(i == 0, ctx0 + b, m(b, i)[0]), 0))
    out_rows = t_all + (batch - 1) * c

    def full(a):
        if a.ndim == 2:
            return pl.BlockSpec(a.shape, lambda b, i: (0, 0))
        return pl.BlockSpec((None,) + a.shape[1:], lambda b, i: (layer,) + (0,) * (a.ndim - 1))

    spec = lambda w, m: pl.BlockSpec((c, w), m)
    state = pltpu.VMEM((RET_QK_WIDTH, RET_V_WIDTH), F32)
    o_f = pl.pallas_call(
        _ret_fwd_kernel,
        grid=(batch, nch + 1),
        in_specs=[spec(RET_QK_WIDTH, fmap), spec(RET_QK_WIDTH, fmap), spec(RET_V_WIDTH, fmap),
                  full(dmat), full(win_f), full(wend_f), full(gch_f), full(bm)],
        out_specs=spec(RET_V_WIDTH, spare(fmap)),
        out_shape=jax.ShapeDtypeStruct((out_rows, RET_V_WIDTH), F32),
        scratch_shapes=[state],
        compiler_params=_cparams("arbitrary", "arbitrary"),
        name="retention_fwd",
    )(qb, kb, vb, dmat, win_f, wend_f, gch_f, bm)
    return pl.pallas_call(
        _ret_bwd_kernel,
        grid=(batch, nch + 1),
        in_specs=[spec(RET_QK_WIDTH, rmap), spec(RET_QK_WIDTH, rmap), spec(RET_V_WIDTH, rmap),
                  spec(RET_V_WIDTH, rmap), spec(RET_V_WIDTH, rmap),
                  full(win_b), full(wend_b), full(gch_b), full(bm), full(gn)],
        out_specs=spec(RET_V_WIDTH, spare(rmap)),
        out_shape=jax.ShapeDtypeStruct((out_rows, RET_V_WIDTH), MXU_DTYPE),
        scratch_shapes=[state],
        compiler_params=_cparams("arbitrary", "arbitrary"),
        name="retention_bwd",
    )(qb, kb, vb, o_f, gb, win_b, wend_b, gch_b, bm, gn)


def _s5_discretize(a_re, a_im, log_dt, b_re, b_im):
    a_re = jnp.minimum(a_re.astype(F32), -1e-4)
    a_im = a_im.astype(F32)
    dt = jnp.exp(log_dt.astype(F32))[..., None]
    mag = jnp.exp(dt * a_re)
    ab_re = mag * jnp.cos(dt * a_im)
    ab_im = mag * jnp.sin(dt * a_im)
    den = a_re * a_re + a_im * a_im
    nr = ab_re - 1.0
    f_re = ((nr * a_re + ab_im * a_im) / den)[..., None]
    f_im = ((ab_im * a_re - nr * a_im) / den)[..., None]
    br = b_re.astype(F32)[None]
    bi = b_im.astype(F32)[None]
    return ab_re, ab_im, f_re * br - f_im * bi, f_re * bi + f_im * br


def _complex_powers(ar, ai, n):
    pr = jnp.ones((1,) + ar.shape, F32)
    pi = jnp.zeros((1,) + ar.shape, F32)
    cr, ci = ar, ai
    while pr.shape[0] < n:
        pr, pi = (jnp.concatenate([pr, pr * cr - pi * ci], axis=0),
                  jnp.concatenate([pi, pr * ci + pi * cr], axis=0))
        cr, ci = cr * cr - ci * ci, 2.0 * cr * ci
    return pr[:n], pi[:n]


def _s5_operators(a_re, a_im, log_dt, b_re, b_im, c_re, c_im, levels):
    G, P, Cg, L = S5_GROUPS, S5_STATE, S5_GROUP_CH, S5_CHUNK
    ab_re, ab_im, bb_re, bb_im = _s5_discretize(a_re, a_im, log_dt, b_re, b_im)
    pr, pi = _complex_powers(ab_re, ab_im, L + 1)
    cr = c_re.astype(F32)
    ci = c_im.astype(F32)
    bt = jnp.concatenate([jnp.swapaxes(bb_re, 2, 3), jnp.swapaxes(bb_im, 2, 3)], axis=3)

    def powers(n, lo, descending):
        sel = lambda z: jnp.flip(z[lo:lo + L, n], axis=0) if descending else z[lo:lo + L, n]
        return sel(pr), sel(pi)

    def state_in(n, lo, descending):
        qr, qi = powers(n, lo, descending)
        wr = jnp.einsum('sgp,gpc->gscp', qr, bb_re[n]) - jnp.einsum('sgp,gpc->gscp', qi, bb_im[n])
        wi = jnp.einsum('sgp,gpc->gscp', qr, bb_im[n]) + jnp.einsum('sgp,gpc->gscp', qi, bb_re[n])
        return jnp.concatenate([wr, wi], axis=-1).reshape(G, L * Cg, 2 * P)

    def state_out(n, lo, descending):
        qr, qi = powers(n, lo, descending)
        er = jnp.einsum('tgp,gcp->gptc', qr, cr[n]) - jnp.einsum('tgp,gcp->gptc', qi, ci[n])
        ei = jnp.einsum('tgp,gcp->gptc', qr, ci[n]) + jnp.einsum('tgp,gcp->gptc', qi, cr[n])
        return jnp.concatenate([er, -ei], axis=1).reshape(G, 2 * P, L * Cg)

    ca = jnp.stack([state_out(0, 0, False), state_out(1, 0, True)])
    f = jnp.concatenate([state_in(0, 0, True), state_in(1, 0, False)], axis=-1).astype(MXU_DTYPE)
    e = jnp.stack([state_out(0, 1, False), state_out(1, 1, True)]).astype(MXU_DTYPE)
    return bt, ca, f, e, _s5_scan_mults(pr[L], pi[L], levels)


def _s5_scan_mults(alr, ali, levels):
    rows = []
    cr, ci = alr, ali
    for _ in range(levels):
        rows.append(jnp.stack([jnp.concatenate([cr, cr], -1), jnp.concatenate([-ci, ci], -1)], axis=-2))
        cr, ci = cr * cr - ci * ci, 2.0 * cr * ci
    return jnp.stack(rows, axis=2)


def _shift_rows(x, k, down):
    n = x.shape[0]
    row = lax.broadcasted_iota(jnp.int32, (n, 1), 0)
    if down:
        return jnp.where(row >= k, pltpu.roll(x, k, 0), 0.0)
    return jnp.where(row < n - k, pltpu.roll(x, n - k, 0), 0.0)


def _swap_halves(x):
    return pltpu.roll(x, S5_STATE, 1)


def _s5_scans(items):
    n = items[0][0].shape[0]
    row = lax.broadcasted_iota(jnp.int32, (n, 1), 0)
    first = [(row == (0 if fwd else n - 1)).astype(F32) for *_, fwd in items]
    times = lambda m_ref, lvl, z: m_ref[lvl, 0:1] * z + m_ref[lvl, 1:2] * _swap_halves(z)
    xs = [loc + f * times(m_ref, 0, init) for (loc, m_ref, init, _), f in zip(items, first)]
    k, lvl = 1, 0
    while k < n:
        shifted = [_shift_rows(x, k, it[3]) for x, it in zip(xs, items)]
        xs = [x + times(it[1], lvl, sx) for x, sx, it in zip(xs, shifted, items)]
        k, lvl = 2 * k, lvl + 1
    out = []
    for x, it, f in zip(xs, items, first):
        last = slice(n - 1, n) if it[3] else slice(0, 1)
        out.append((_shift_rows(x, 1, it[3]) + f * it[2], x[last]))
    return out


def _s5_toeplitz(bt_ref, ca_ref, conv_ref):
    L, Cg = S5_CHUNK, S5_GROUP_CH
    width = L * Cg
    hi = lax.Precision.HIGHEST
    kf = jnp.dot(bt_ref[0], ca_ref[0], preferred_element_type=F32, precision=hi)
    kb = jnp.dot(bt_ref[1], ca_ref[1], preferred_element_type=F32, precision=hi)
    lane = lax.broadcasted_iota(jnp.int32, (Cg, width), 1)
    for s in range(L):
        right = s * Cg
        left = (L - 1 - s) * Cg
        fwd = kf if right == 0 else jnp.where(lane >= right, pltpu.roll(kf, right, 1), 0.0)
        bwd = kb if left == 0 else jnp.where(lane < width - left, pltpu.roll(kb, width - left, 1), 0.0)
        conv_ref[s * Cg:(s + 1) * Cg, :] = fwd + bwd


def _s5_kernel(u_ref, bt_ref, ca_ref, fcat_ref, e_ref, mult_ref, d_ref, y_ref, conv_ref, *, batch, n_lat_ch, n_ctx_ch):
    P2 = 2 * S5_STATE
    u = u_ref[...]
    ub = u.astype(MXU_DTYPE)
    loc = _mxu(ub, fcat_ref[...])
    _s5_toeplitz(bt_ref, ca_ref, conv_ref)
    y_ref[...] = _mxu(ub, conv_ref[...]) + u * d_ref[...]
    zero = jnp.zeros((1, P2), F32)
    chains = [(b, n) for b in range(batch) for n in range(2)]
    ctx_rows = lambda b: slice(batch * n_lat_ch + b * n_ctx_ch, batch * n_lat_ch + (b + 1) * n_ctx_ch)
    lat_rows = lambda b: slice(b * n_lat_ch, (b + 1) * n_lat_ch)
    item = lambda rows, n, init: (loc[rows, n * P2:(n + 1) * P2], mult_ref.at[n], init, n == 0)
    ctx_out = _s5_scans([item(ctx_rows(b), n, zero) for b, n in chains])
    lat_out = _s5_scans([item(lat_rows(b), n, x_last) for (b, n), (_, x_last) in zip(chains, ctx_out)])
    for (b, n), (prev_c, _), (prev_l, _) in zip(chains, ctx_out, lat_out):
        y_ref[ctx_rows(b), :] += _mxu(prev_c, e_ref[n])
        y_ref[lat_rows(b), :] += _mxu(prev_l, e_ref[n])


def _s5_mixer(ug, ops, d_skip, layer, batch, n_lat, n_ctx):
    bt, ca, f_cat, e, mults = ops
    G, Cg, L, P2 = S5_GROUPS, S5_GROUP_CH, S5_CHUNK, 2 * S5_STATE
    rows, width = ug.shape[1], L * Cg
    levels = mults.shape[3]
    per_g = lambda *s: pl.BlockSpec((None,) + s, lambda g: (g,) + (0,) * len(s))
    per_dir = lambda *s: pl.BlockSpec((None, 2, None) + s, lambda g: (layer, 0, g) + (0,) * len(s))
    return pl.pallas_call(
        functools.partial(_s5_kernel, batch=batch, n_lat_ch=n_lat // L, n_ctx_ch=n_ctx // L),
        grid=(G,),
        in_specs=[per_g(rows, width), per_dir(Cg, P2), per_dir(P2, width),
                  pl.BlockSpec((None, None, width, 2 * P2), lambda g: (layer, g, 0, 0)),
                  per_dir(P2, width), per_dir(levels, 2, P2),
                  pl.BlockSpec((None, None, 1, width), lambda g: (layer, g, 0, 0))],
        out_specs=per_g(rows, width),
        out_shape=jax.ShapeDtypeStruct((G, rows, width), F32),
        scratch_shapes=[pltpu.VMEM((width, width), F32)],
        compiler_params=_cparams("arbitrary"),
        name="s5_conv",
    )(ug, bt, ca, f_cat, e, mults, d_skip)


def _mix_ffn_kernel(x_ref, mod_ref, ya_ref, yb_ref, yg_ref, wglu_ref, wo_ref, win_ref, wout_ref, o_ref, tok_ref,
                    *, tiles_per_batch):
    d = x_ref.shape[1]
    row = pl.program_id(0) // tiles_per_batch
    mix_ab = (_mxu(ya_ref[...], wo_ref[0:NA_WIDTH, :])
              + _mxu(yb_ref[...], wo_ref[NA_WIDTH:NA_WIDTH + RET_V_WIDTH, :]))
    _groups_to_tokens(yg_ref, tok_ref)
    yc = jnp.concatenate([tok_ref[hb] for hb in range(S5_WIDTH // LANES)], axis=1)
    glu = _mxu(jax.nn.gelu(yc), wglu_ref[...])
    s5 = glu[:, :S5_WIDTH] * jax.nn.sigmoid(glu[:, S5_WIDTH:])
    mix = mix_ab + _mxu(s5, wo_ref[NA_WIDTH + RET_V_WIDTH:, :])
    x = x_ref[...] + _mod_rows(mod_ref, row, 5, d) * mix
    o_ref[...] = _ffn_body(x, mod_ref, row, 6, win_ref, wout_ref)


def _mix_ffn(x_all, mods_l, ya, yb, yg, w_glu, w_out, w_in2, w_out2, layer, tiles_per_batch, n_tiles):
    d = x_all.shape[1]
    tm = TOKEN_TILE
    tok = lambda w: pl.BlockSpec((tm, w), lambda i: (i, 0))
    return pl.pallas_call(
        functools.partial(_mix_ffn_kernel, tiles_per_batch=tiles_per_batch),
        grid=(n_tiles,),
        in_specs=[tok(d), pl.BlockSpec(mods_l.shape, lambda i: (0, 0)), tok(NA_WIDTH), tok(RET_V_WIDTH),
                  pl.BlockSpec((S5_GROUPS, tm // S5_CHUNK, yg.shape[2]), lambda i: (0, i, 0)),
                  _resident(w_glu.shape, layer), _resident(w_out.shape, layer),
                  _resident(w_in2.shape, layer), _resident(w_out2.shape, layer)],
        out_specs=tok(d),
        out_shape=jax.ShapeDtypeStruct((n_tiles * tm, d), F32),
        scratch_shapes=[_token_scratch(tm)],
        compiler_params=_cparams("arbitrary"),
        name="mix_ffn2",
    )(x_all, mods_l, ya, yb, yg, w_glu, w_out, w_in2, w_out2)


def _rope_tables(n_lat, n_ctx):
    nf = RET_QK_DIM // 4
    inv = ROPE_BASE ** (-jnp.arange(nf, dtype=F32) / nf)
    t = jnp.arange(n_lat)
    row = (t // GRID_W).astype(F32)
    col = (t % GRID_W).astype(F32)
    ang = jnp.concatenate([row[:, None] * inv, col[:, None] * inv], axis=-1)
    cos, sin = jnp.cos(ang), jnp.sin(ang)
    cos_l = jnp.tile(jnp.concatenate([cos, cos], axis=-1), (1, RET_HEADS))
    sin_l = jnp.tile(jnp.concatenate([-sin, sin], axis=-1), (1, RET_HEADS))
    pad = TOKEN_TILE
    return (jnp.concatenate([cos_l, jnp.ones((pad, RET_QK_WIDTH), F32)], axis=0),
            jnp.concatenate([sin_l, jnp.zeros((pad, RET_QK_WIDTH), F32)], axis=0))


def kernel(x, c, ctx, c_ctx, w_mod, b_mod, ffn1_w_in, ffn1_w_out, w_in, w_out, na_q_gain, na_k_gain, na_rpb,
           ret_decay, ret_gn, s5_a_re, s5_a_im, s5_log_dt, s5_b_re, s5_b_im, s5_c_re, s5_c_im, s5_d, s5_w_glu,
           ffn2_w_in, ffn2_w_out):
    batch, n_lat, d = x.shape
    n_ctx = ctx.shape[1]
    depth = w_mod.shape[0]
    assert n_ctx == NA_BLOCK == RET_CHUNK and batch * n_ctx == TOKEN_TILE
    assert n_lat % TOKEN_TILE == 0 and n_lat // NA_BLOCK >= 4
    assert batch + 1 <= 8
    tiles_per_batch = n_lat // TOKEN_TILE
    n_lat_tiles = batch * tiles_per_batch
    rows = n_lat // GRID_W

    bf = lambda a: a.astype(MXU_DTYPE)
    ffn1_w_in, ffn1_w_out, ffn2_w_in, ffn2_w_out = bf(ffn1_w_in), bf(ffn1_w_out), bf(ffn2_w_in), bf(ffn2_w_out)
    w_in, w_out, s5_w_glu = bf(w_in), bf(w_out), bf(s5_w_glu)

    cond = jnp.concatenate([c, c_ctx[None, :], jnp.zeros((8 - batch - 1, d), F32)], axis=0)
    mods = _modulation(cond, w_mod, b_mod)

    head_sum = jnp.asarray(np.arange(NA_WIDTH)[:, None] // NA_HEAD_DIM == np.arange(NA_WIDTH)[None, :] // NA_HEAD_DIM,
                           MXU_DTYPE)
    cos_tab, sin_tab = _rope_tables(n_lat, n_ctx)
    levels = max(1, int(math.ceil(math.log2(max(n_lat, n_ctx) // S5_CHUNK))))
    s5_ops = jax.vmap(functools.partial(_s5_operators, levels=levels))(
        s5_a_re, s5_a_im, s5_log_dt, s5_b_re, s5_b_im, s5_c_re, s5_c_im)

    na_bias, na_window = _na_bias_tables(na_rpb, rows)
    ret_tables = jax.vmap(_ret_tables)(ret_decay)
    ret_gain = ret_gn.astype(F32).reshape(depth, 1, RET_V_WIDTH)
    q_gain = jnp.tile(na_q_gain.astype(F32), (1, NA_HEADS)).reshape(depth, 1, NA_WIDTH)
    k_gain = jnp.tile(na_k_gain.astype(F32), (1, NA_HEADS)).reshape(depth, 1, NA_WIDTH)
    s5_skip = jnp.tile(s5_d.astype(F32).reshape(depth, S5_GROUPS, 1, S5_GROUP_CH), (1, 1, 1, S5_CHUNK))

    x_all = x.reshape(batch * n_lat, d)
    x_ctx = ctx.reshape(batch * n_ctx, d)
    for l in range(depth):
        mods_l = mods[l]
        x_all = _ffn_first(x_all, x_ctx if l == 0 else None, mods_l, ffn1_w_in, ffn1_w_out, l, tiles_per_batch)
        qa, ka, va, qb, kb, vb, gb, ug = _inproj(x_all, mods_l, w_in, l, head_sum, q_gain[l], k_gain[l],
                                                 cos_tab, sin_tab, tiles_per_batch, n_lat_tiles)
        ya = _neighborhood_attention(qa, ka, va, na_bias, na_window, l, batch, n_lat)
        yb = _retention(qb, kb, vb, gb, ret_tables, ret_gain, l, batch, n_lat)
        yg = _s5_mixer(ug, s5_ops, s5_skip, l, batch, n_lat, n_ctx)
        n_tiles = n_lat_tiles + 1 if l < depth - 1 else n_lat_tiles
        x_all = _mix_ffn(x_all, mods_l, ya, yb, yg, s5_w_glu, w_out, ffn2_w_in, ffn2_w_out, l, tiles_per_batch, n_tiles)
    return x_all.reshape(batch, n_lat, d)
```
